```python
import math
import jax, jax.numpy as jnp
from jax import lax
import numpy as np

D_MODEL = 2048
BATCH = 4
SEQ = 2048
DEPTH = 4
DEC_BATCH = 128
DEC_SEQ = 8
PAST_LEN = 16384
PAGE_SIZE = 128

N_META = 16
MIX_WIDTH = D_MODEL
GDN_HEAD_DIM = 128
GDN_HEADS = (MIX_WIDTH // 2) // GDN_HEAD_DIM
GDN_KEY_W = GDN_HEADS * GDN_HEAD_DIM
GDN_VAL_W = GDN_HEADS * GDN_HEAD_DIM
HG_HEAD_DIM = 128
HG_EXPAND = 128
HG_HEADS = (MIX_WIDTH - GDN_VAL_W) // HG_HEAD_DIM
HG_KEY_W = HG_HEADS * HG_EXPAND
HG_VAL_W = HG_HEADS * HG_HEAD_DIM
CONV_W = 4
CHUNK = 64
D_FF = 4 * D_MODEL
EPS = 1e-6
SPLIT_SIZES = (GDN_KEY_W, GDN_KEY_W, GDN_VAL_W, GDN_VAL_W, GDN_HEADS, GDN_HEADS,
               HG_KEY_W, HG_KEY_W, HG_VAL_W, HG_VAL_W)
N_IN = 2 * GDN_KEY_W + 2 * GDN_VAL_W + 2 * GDN_HEADS + 2 * HG_KEY_W + 2 * HG_VAL_W

kernel_name = "hymba_gdn_hgrn2_decoder_step"


def rmsnorm(x, w):
    xf = x.astype(jnp.float32)
    y = xf * lax.rsqrt(jnp.mean(xf * xf, axis=-1, keepdims=True) + EPS)
    return (y * w.astype(jnp.float32)).astype(x.dtype)


def l2norm(x):
    return x * lax.rsqrt(jnp.sum(x * x, axis=-1, keepdims=True) + EPS)


def gated_head_norm(o, z, w):
    y = o * lax.rsqrt(jnp.mean(o * o, axis=-1, keepdims=True) + EPS)
    return y * w.astype(jnp.float32) * jax.nn.silu(z.astype(jnp.float32))


def causal_conv(u, buf, w):
    full = jnp.concatenate([buf.astype(u.dtype), u], axis=1)
    C = u.shape[-1]
    out = lax.conv_general_dilated(full, w.astype(u.dtype)[:, None, :], window_strides=(1,),
                                   padding='VALID', dimension_numbers=('NWC', 'WIO', 'NWC'),
                                   feature_group_count=C)
    return jax.nn.silu(out), full[:, -(CONV_W - 1):]


def _chunks(t, chunk):
    B, L = t.shape[:2]
    t = t.reshape((B, L // chunk, chunk) + t.shape[2:])
    perm = (1, 0, 3, 2) + tuple(range(4, t.ndim))
    return t.transpose(perm)


def _unchunk(t):
    n, B, H, C, V = t.shape
    return t.transpose(1, 0, 3, 2, 4).reshape(B, n * C, H, V)


def gdn_chunked(q, k, v, beta, g, S0, chunk):
    V = v.shape[-1]
    incl = jnp.tril(jnp.ones((chunk, chunk), bool))
    strict = jnp.tril(jnp.ones((chunk, chunk), bool), -1)
    eye = jnp.eye(chunk, dtype=jnp.float32)

    def step(S, xs):
        qc, kc, vc, bc, gc = xs
        b = jnp.cumsum(gc, axis=-1)
        diff = b[..., :, None] - b[..., None, :]
        decay = jnp.exp(jnp.where(incl, diff, -jnp.inf))
        kb = kc * bc[..., None]
        A = jnp.where(strict, jnp.einsum('bhtk,bhsk->bhts', kb, kc) * decay, 0.0)
        rhs = jnp.concatenate([vc * bc[..., None], kb * jnp.exp(b)[..., None]], axis=-1)
        sol = lax.linalg.triangular_solve(A + eye, rhs, left_side=True, lower=True,
                                          unit_diagonal=True)
        U, W = sol[..., :V], sol[..., V:]
        v_new = U - jnp.einsum('bhck,bhkv->bhcv', W, S)
        qk = jnp.einsum('bhtk,bhsk->bhts', qc, kc) * decay
        o = (jnp.einsum('bhck,bhkv->bhcv', qc * jnp.exp(b)[..., None], S)
             + jnp.einsum('bhts,bhsv->bhtv', qk, v_new))
        bl = b[..., -1:]
        S_new = (jnp.exp(bl)[..., None] * S
                 + jnp.einsum('bhck,bhcv->bhkv', kc * jnp.exp(bl - b)[..., None], v_new))
        return S_new, o

    xs = (_chunks(q, chunk), _chunks(k, chunk), _chunks(v, chunk),
          _chunks(beta, chunk), _chunks(g, chunk))
    S, o = lax.scan(step, S0, xs)
    return S, _unchunk(o)


def hgrn2_chunked(q, k, v, g, S0, chunk):
    incl = jnp.tril(jnp.ones((chunk, chunk), bool))[..., None]

    def step(S, xs):
        qc, kc, vc, gc = xs
        b = jnp.cumsum(gc, axis=-2)
        diff = b[..., :, None, :] - b[..., None, :, :]
        decay = jnp.exp(jnp.where(incl, diff, -jnp.inf))
        A = jnp.einsum('bhtk,bhsk,bhtsk->bhts', qc, kc, decay)
        o = (jnp.einsum('bhck,bhkv->bhcv', qc * jnp.exp(b), S)
             + jnp.einsum('bhts,bhsv->bhtv', A, vc))
        bl = b[..., -1:, :]
        S_new = (jnp.exp(bl[..., 0, :])[..., None] * S
                 + jnp.einsum('bhck,bhcv->bhkv', kc * jnp.exp(bl - b), vc))
        return S_new, o

    xs = (_chunks(q, chunk), _chunks(k, chunk), _chunks(v, chunk), _chunks(g, chunk))
    S, o = lax.scan(step, S0, xs)
    return S, _unchunk(o)


def token_mixer(h, conv_buf, s_gdn, s_hg, lb, w_in, conv_w, a_log, dt_bias,
                gdn_norm_w, hg_norm_w, w_out, segments):
    B, L, _ = h.shape
    f32 = jnp.float32
    proj = h @ w_in
    parts, off = [], 0
    for s in SPLIT_SIZES:
        parts.append(proj[..., off:off + s])
        off += s
    qg, kg, vg, zg, bg, ag, qh, fh, ih, zh = parts
    qkv, new_buf = causal_conv(jnp.concatenate([qg, kg, vg], axis=-1), conv_buf, conv_w)
    qkv = qkv.astype(f32)
    q = l2norm(qkv[..., :GDN_KEY_W].reshape(B, L, GDN_HEADS, GDN_HEAD_DIM)) * (GDN_HEAD_DIM ** -0.5)
    k = l2norm(qkv[..., GDN_KEY_W:2 * GDN_KEY_W].reshape(B, L, GDN_HEADS, GDN_HEAD_DIM))
    v = qkv[..., 2 * GDN_KEY_W:].reshape(B, L, GDN_HEADS, GDN_HEAD_DIM)
    beta = jax.nn.sigmoid(bg.astype(f32))
    g = -jnp.exp(a_log.astype(f32)) * jax.nn.softplus(ag.astype(f32) + dt_bias.astype(f32))
    fgate = lb + (1.0 - lb) * jax.nn.sigmoid(fh.astype(f32))
    gh = jnp.log(fgate).reshape(B, L, HG_HEADS, HG_EXPAND)
    kh = (1.0 - fgate).reshape(B, L, HG_HEADS, HG_EXPAND)
    qhh = qh.astype(f32).reshape(B, L, HG_HEADS, HG_EXPAND)
    vh = ih.astype(f32).reshape(B, L, HG_HEADS, HG_HEAD_DIM)
    s_gdn = s_gdn.astype(f32)
    s_hg = s_hg.astype(f32)
    og_list, oh_list, start = [], [], 0
    for length, chunk in segments:
        sl = slice(start, start + length)
        s_gdn, o1 = gdn_chunked(q[:, sl], k[:, sl], v[:, sl], beta[:, sl], g[:, sl], s_gdn, chunk)
        s_hg, o2 = hgrn2_chunked(qhh[:, sl], kh[:, sl], vh[:, sl], gh[:, sl], s_hg, chunk)
        og_list.append(o1)
        oh_list.append(o2)
        start += length
    og = gated_head_norm(jnp.concatenate(og_list, axis=1),
                         zg.reshape(B, L, GDN_HEADS, GDN_HEAD_DIM), gdn_norm_w)
    oh = gated_head_norm(jnp.concatenate(oh_list, axis=1),
                         zh.reshape(B, L, HG_HEADS, HG_HEAD_DIM), hg_norm_w)
    mix = jnp.concatenate([og.reshape(B, L, GDN_VAL_W), oh.reshape(B, L, HG_VAL_W)],
                          axis=-1).astype(h.dtype)
    return mix @ w_out, new_buf, s_gdn, s_hg


def run_trunk(x, conv_st, gdn_st, hg_st, segments, norm1_w, w_in, conv_w, a_log, dt_bias,
              gdn_norm_w, hgrn_lb, hgrn_norm_w, w_out, norm2_w, w_up, w_down, final_norm_w):
    p = jax.nn.softmax(hgrn_lb.astype(jnp.float32), axis=0)
    lb_all = jnp.cumsum(p, axis=0) - p[0]
    new_conv, new_gdn, new_hg = [], [], []
    for l in range(DEPTH):
        h = rmsnorm(x, norm1_w[l])
        mix, c, sg, sh = token_mixer(h, conv_st[l], gdn_st[l], hg_st[l], lb_all[l], w_in[l],
                                     conv_w[l], a_log[l], dt_bias[l], gdn_norm_w[l],
                                     hgrn_norm_w[l], w_out[l], segments)
        x = x + mix
        u = jax.nn.relu(rmsnorm(x, norm2_w[l]) @ w_up[l])
        x = x + (u * u) @ w_down[l]
        new_conv.append(c)
        new_gdn.append(sg)
        new_hg.append(sh)
    return (rmsnorm(x, final_norm_w), jnp.stack(new_conv), jnp.stack(new_gdn), jnp.stack(new_hg))


def setup_inputs(seed: int = 0) -> dict:
    key = jax.random.key(seed)
    ks = jax.random.split(key, 20)
    f32 = jnp.float32
    nrm = lambda k, shape, s: jax.random.normal(k, shape, f32) * s
    dt = jnp.exp(jax.random.uniform(ks[9], (DEPTH, GDN_HEADS), f32)
                 * (math.log(0.1) - math.log(0.001)) + math.log(0.001))
    return {
        "x_prompt": nrm(ks[0], (BATCH, SEQ, D_MODEL), 1.0),
        "x_sample": nrm(ks[1], (DEC_BATCH, DEC_SEQ, D_MODEL), 1.0),
        "state_gdn_conv": nrm(ks[2], (DEPTH, DEC_BATCH, CONV_W - 1, 3 * GDN_KEY_W), 1.0),
        "state_gdn": nrm(ks[3], (DEPTH, DEC_BATCH, GDN_HEADS, GDN_HEAD_DIM, GDN_HEAD_DIM), 0.1),
        "state_hgrn": nrm(ks[4], (DEPTH, DEC_BATCH, HG_HEADS, HG_EXPAND, HG_HEAD_DIM), 0.5),
        "meta_tokens": nrm(ks[5], (N_META, D_MODEL), 1.0),
        "norm1_w": 1.0 + nrm(ks[6], (DEPTH, D_MODEL), 0.02),
        "w_in": nrm(ks[7], (DEPTH, D_MODEL, N_IN), D_MODEL ** -0.5),
        "conv_w": nrm(ks[8], (DEPTH, CONV_W, 3 * GDN_KEY_W), 0.5),
        "a_log": jnp.log(jax.random.uniform(ks[10], (DEPTH, GDN_HEADS), f32, 1.0, 16.0)),
        "dt_bias": dt + jnp.log(-jnp.expm1(-dt)),
        "gdn_norm_w": 1.0 + nrm(ks[11], (DEPTH, GDN_HEAD_DIM), 0.02),
        "hgrn_lb": nrm(ks[12], (DEPTH, HG_KEY_W), 0.1),
        "hgrn_norm_w": 1.0 + nrm(ks[13], (DEPTH, HG_HEAD_DIM), 0.02),
        "w_out": nrm(ks[14], (DEPTH, MIX_WIDTH, D_MODEL), MIX_WIDTH ** -0.5),
        "norm2_w": 1.0 + nrm(ks[15], (DEPTH, D_MODEL), 0.02),
        "w_up": nrm(ks[16], (DEPTH, D_MODEL, D_FF), D_MODEL ** -0.5),
        "w_down": nrm(ks[17], (DEPTH, D_FF, D_MODEL), D_FF ** -0.5),
        "final_norm_w": 1.0 + nrm(ks[18], (D_MODEL,), 0.02),
    }


def reference(x_prompt, x_sample, state_gdn_conv, state_gdn, state_hgrn, meta_tokens, norm1_w,
              w_in, conv_w, a_log, dt_bias, gdn_norm_w, hgrn_lb, hgrn_norm_w, w_out, norm2_w,
              w_up, w_down, final_norm_w):
    B = x_prompt.shape[0]
    xp = jnp.concatenate([jnp.broadcast_to(meta_tokens.astype(x_prompt.dtype)[None],
                                           (B, N_META, D_MODEL)), x_prompt], axis=1)
    zc = jnp.zeros((DEPTH, B, CONV_W - 1, 3 * GDN_KEY_W), x_prompt.dtype)
    zg = jnp.zeros((DEPTH, B, GDN_HEADS, GDN_HEAD_DIM, GDN_HEAD_DIM), jnp.float32)
    zh = jnp.zeros((DEPTH, B, HG_HEADS, HG_EXPAND, HG_HEAD_DIM), jnp.float32)
    yp, pc, pg, ph = run_trunk(xp, zc, zg, zh, [(N_META, N_META), (SEQ, CHUNK)], norm1_w, w_in,
                               conv_w, a_log, dt_bias, gdn_norm_w, hgrn_lb, hgrn_norm_w, w_out,
                               norm2_w, w_up, w_down, final_norm_w)
    ys, sc, sg, sh = run_trunk(x_sample, state_gdn_conv, state_gdn, state_hgrn,
                               [(DEC_SEQ, DEC_SEQ)], norm1_w, w_in, conv_w, a_log, dt_bias,
                               gdn_norm_w, hgrn_lb, hgrn_norm_w, w_out, norm2_w, w_up, w_down,
                               final_norm_w)
    return (yp[:, N_META:], ys, pc.astype(x_prompt.dtype), pg.astype(state_gdn.dtype),
            ph.astype(state_hgrn.dtype), sc.astype(state_gdn_conv.dtype),
            sg.astype(state_gdn.dtype), sh.astype(state_hgrn.dtype))
```

```python
import functools
import math

import numpy as np
import jax
import jax.numpy as jnp
from jax import lax
from jax.experimental import pallas as pl
from jax.experimental.pallas import tpu as pltpu

HEAD_DIM = 128
CHUNK = 64
CONV_W = 4
EPS = 1e-6
SAMPLE_SEQS_PER_STEP = 2
V7X_VMEM_LIMIT_BYTES = 56 * 1024 * 1024

F32 = jnp.float32
BF16 = jnp.bfloat16


def _mm(a, b):
    return jnp.dot(a.astype(BF16), b.astype(BF16), preferred_element_type=F32)


def _mm_nt(a, b):
    return lax.dot_general(a.astype(BF16), b.astype(BF16), (((1,), (1,)), ((), ())),
                           preferred_element_type=F32)


def _mm_tn(a, b):
    return lax.dot_general(a.astype(BF16), b.astype(BF16), (((0,), (0,)), ((), ())),
                           preferred_element_type=F32)


def _mm_f32(a, b):
    return jnp.dot(a, b, preferred_element_type=F32, precision=lax.Precision.HIGHEST)


def _mm_sel(m01, x):
    x1 = x.astype(BF16)
    r1 = x - x1.astype(F32)
    x2 = r1.astype(BF16)
    x3 = (r1 - x2.astype(F32)).astype(BF16)
    return (jnp.dot(m01, x1, preferred_element_type=F32)
            + jnp.dot(m01, x2, preferred_element_type=F32)
            + jnp.dot(m01, x3, preferred_element_type=F32))


def _sigmoid(x):
    return 1.0 / (1.0 + jnp.exp(-x))


def _silu(x):
    return x * _sigmoid(x)


def _softplus(x):
    return jnp.maximum(x, 0.0) + jnp.log(1.0 + jnp.exp(-jnp.abs(x)))


def _pick_tile(m, target, mult=16):
    best = None
    for t in range(mult, m + 1, mult):
        if m % t == 0 and t <= target:
            best = t
    if best is None:
        raise ValueError(f"no tile for {m}")
    return best


def _cparams(sem):
    return pltpu.CompilerParams(dimension_semantics=sem, vmem_limit_bytes=V7X_VMEM_LIMIT_BYTES)


@functools.lru_cache(maxsize=None)
def _block_constants(groups, toks):
    rows = groups * toks
    grp = np.arange(rows) // toks
    pos = np.arange(rows) % toks
    same = grp[:, None] == grp[None, :]
    pt, ps = pos[:, None], pos[None, :]
    mats = [same & (ps <= pt), same & (ps > pt)]
    lev = np.full((rows, rows), -1, np.int32)
    lev[np.arange(rows), np.arange(rows)] = 0
    n, idx = toks, 1
    while n >= 2:
        half = n // 2
        blk = same & ((pt // n) == (ps // n))
        qt, qs = pt % n, ps % n
        hi_row = blk & (qt >= half) & (qs >= half) & (qs <= qt)
        lo_row = blk & (qt < half) & (qs > qt) & (qs < half)
        mats.append(hi_row | lo_row)
        lev[blk & (qt >= half) & (qs < half)] = idx
        n //= 2
        idx += 1
    sel = np.concatenate(mats, axis=0).astype(np.float32)
    return sel, lev, idx - 1


def _gdn_block(q, k, v, beta, bcol, brow, sufcol, lev, states, toks):
    rows = q.shape[0]
    groups = rows // toks
    decay = jnp.exp(jnp.where(lev >= 0, bcol - brow, -jnp.inf))
    kb = k * beta
    a = jnp.where(lev >= 1, _mm_nt(kb, k) * decay, 0.0)
    inv = (lev == 0).astype(F32) - a
    apow = a
    for _ in range(int(math.log2(toks)) - 1):
        apow = _mm_f32(apow, apow)
        inv = inv + _mm_f32(inv, apow)
    eb = jnp.exp(bcol)
    sol = _mm_f32(inv, jnp.concatenate([v * beta, kb * eb], axis=1))
    u, w = sol[:, :HEAD_DIM], sol[:, HEAD_DIM:]
    qe = q * eb
    kdec = k * jnp.exp(sufcol)
    ws, os_ = [], []
    for g in range(groups):
        sl = slice(g * toks, (g + 1) * toks)
        ws.append(_mm(w[sl], states[g]))
        os_.append(_mm(qe[sl], states[g]))
    v_new = u - jnp.concatenate(ws, axis=0)
    qk = _mm_nt(q, k) * decay
    o = jnp.concatenate(os_, axis=0) + _mm(qk, v_new)
    new_states = []
    for g in range(groups):
        sl = slice(g * toks, (g + 1) * toks)
        r0 = g * toks
        ebl = jnp.exp(bcol[r0:r0 + 1] + sufcol[r0:r0 + 1])
        new_states.append(ebl * states[g] + _mm_tn(kdec[sl], v_new[sl]))
    return o, new_states


def _hgrn_block(q, f_logit, v, lb, sel, lev, n_levels, states_t, toks):
    rows = q.shape[0]
    groups = rows // toks
    fg = lb + (1.0 - lb) * _sigmoid(f_logit)
    g_log = jnp.log(fg)
    k = 1.0 - fg
    x = jnp.exp(_mm_sel(sel, g_log))
    x_pre, x_suf = x[0:rows], x[rows:2 * rows]
    a = jnp.where(lev == 0, _mm_nt(q, k), 0.0)
    for i in range(n_levels):
        xl = x[(2 + i) * rows:(3 + i) * rows]
        a = a + jnp.where(lev == i + 1, _mm_nt(q * xl, k * xl), 0.0)
    qb = q * x_pre
    kdec = k * x_suf
    os_ = []
    for g in range(groups):
        sl = slice(g * toks, (g + 1) * toks)
        os_.append(_mm_nt(qb[sl], states_t[g]))
    o = jnp.concatenate(os_, axis=0) + _mm(a, v)
    new_states = []
    for g in range(groups):
        sl = slice(g * toks, (g + 1) * toks)
        last = g * toks + toks - 1
        new_states.append(states_t[g] * x_pre[last:last + 1] + _mm_tn(v[sl], kdec[sl]))
    return o, new_states


def _head_norm_gate(o, z, w):
    y = o * lax.rsqrt(jnp.mean(o * o, axis=-1, keepdims=True) + EPS)
    return y * w * _silu(z)


def _l2norm(x):
    return x * lax.rsqrt(jnp.sum(x * x, axis=-1, keepdims=True) + EPS)


def _layer_lower_bound(lbp, layer):
    m = jnp.max(lbp, axis=0, keepdims=True)
    e = jnp.exp(lbp - m)
    p = e / jnp.sum(e, axis=0, keepdims=True)
    if layer == 0:
        return jnp.zeros_like(m)
    return jnp.sum(p[1:layer + 1], axis=0, keepdims=True)


def _conv_slab(cbuf, base, rows, cols, w):
    y = (w[3:4] * cbuf[base:base + rows, cols] + w[2:3] * cbuf[base - 1:base - 1 + rows, cols]
         + w[1:2] * cbuf[base - 2:base - 2 + rows, cols] + w[0:1] * cbuf[base - 3:base - 3 + rows, cols])
    return _silu(y)


def _mix_prompt_kernel(proj_ref, ba_ref, convw_ref, alog_ref, dtb_ref, gnw_ref, hnw_ref, lbp_ref,
                       sel_ref, lev_ref, mix_ref, convo_ref, sgo_ref, sho_ref,
                       cbuf, sg, sht, *, heads, layer, n_levels):
    c = pl.program_id(1)
    kw = heads * HEAD_DIM
    rows = CHUNK

    @pl.when(c == 0)
    def _():
        cbuf[0:8, :] = jnp.zeros((8, 3 * kw), F32)
        sg[...] = jnp.zeros_like(sg)
        sht[...] = jnp.zeros_like(sht)

    cbuf[8:8 + rows, :] = proj_ref[:, 0:3 * kw]
    sel = sel_ref[...]
    lev = lev_ref[...]
    ba = ba_ref[...]
    beta_all = _sigmoid(ba)
    g_all = -jnp.exp(alog_ref[...]) * _softplus(ba + dtb_ref[...])
    e_g = _mm_sel(sel[0:2 * rows], g_all)
    b_all, suf_all = e_g[0:rows], e_g[rows:2 * rows]
    b_all_t = b_all.T
    gnw = gnw_ref[...]
    hnw = hnw_ref[...]
    scale = HEAD_DIM ** -0.5

    for h in range(heads):
        cq = slice(h * HEAD_DIM, (h + 1) * HEAD_DIM)
        ck = slice(kw + h * HEAD_DIM, kw + (h + 1) * HEAD_DIM)
        cv = slice(2 * kw + h * HEAD_DIM, 2 * kw + (h + 1) * HEAD_DIM)
        q = _l2norm(_conv_slab(cbuf, 8, rows, cq, convw_ref[:, cq])) * scale
        k = _l2norm(_conv_slab(cbuf, 8, rows, ck, convw_ref[:, ck]))
        v = _conv_slab(cbuf, 8, rows, cv, convw_ref[:, cv])
        lane = heads + h
        o, new = _gdn_block(q, k, v, beta_all[:, h:h + 1], b_all[:, lane:lane + 1],
                            b_all_t[lane:lane + 1, :], suf_all[:, lane:lane + 1], lev, [sg[h]], rows)
        sg[h] = new[0]
        z = proj_ref[:, 3 * kw + h * HEAD_DIM:3 * kw + (h + 1) * HEAD_DIM]
        mix_ref[:, cq] = _head_norm_gate(o, z, gnw).astype(mix_ref.dtype)

    tail = cbuf[8 + rows - 3:8 + rows, :]
    cbuf[5:8, :] = tail
    convo_ref[0] = tail

    lb_all = _layer_lower_bound(lbp_ref[...], layer)
    for h in range(heads):
        base = 4 * kw
        ch = slice(h * HEAD_DIM, (h + 1) * HEAD_DIM)
        q = proj_ref[:, base + h * HEAD_DIM:base + (h + 1) * HEAD_DIM]
        f_logit = proj_ref[:, base + kw + h * HEAD_DIM:base + kw + (h + 1) * HEAD_DIM]
        v = proj_ref[:, base + 2 * kw + h * HEAD_DIM:base + 2 * kw + (h + 1) * HEAD_DIM]
        z = proj_ref[:, base + 3 * kw + h * HEAD_DIM:base + 3 * kw + (h + 1) * HEAD_DIM]
        o, new = _hgrn_block(q, f_logit, v, lb_all[:, ch], sel, lev, n_levels, [sht[h]], rows)
        sht[h] = new[0]
        mix_ref[:, kw + h * HEAD_DIM:kw + (h + 1) * HEAD_DIM] = _head_norm_gate(o, z, hnw).astype(mix_ref.dtype)

    @pl.when(c == pl.num_programs(1) - 1)
    def _():
        sgo_ref[0] = sg[...]
        for h in range(heads):
            sho_ref[0, h] = sht[h].T


def _mix_sample_kernel(proj_ref, ba_ref, convs_ref, sgi_ref, shi_ref, convw_ref, alog_ref, dtb_ref,
                       gnw_ref, hnw_ref, lbp_ref, sel_ref, lev_ref,
                       mix_ref, convo_ref, sgo_ref, sho_ref, cbuf, *, heads, layer, n_levels, toks, seqs):
    kw = heads * HEAD_DIM
    sel = sel_ref[...]
    lev = lev_ref[...]
    rows = heads * toks
    gnw = gnw_ref[...]
    hnw = hnw_ref[...]
    scale = HEAD_DIM ** -0.5
    lb_all = _layer_lower_bound(lbp_ref[...], layer)
    lb_rows = jnp.concatenate(
        [jnp.broadcast_to(lb_all[:, h * HEAD_DIM:(h + 1) * HEAD_DIM], (toks, HEAD_DIM)) for h in range(heads)],
        axis=0)

    for s in range(seqs):
        tr = slice(s * toks, (s + 1) * toks)
        cbuf[s, 5:8, :] = convs_ref[0, s]
        cbuf[s, 8:8 + toks, :] = proj_ref[tr, 0:3 * kw]

        def stack(off):
            return jnp.concatenate(
                [proj_ref[tr, off + h * HEAD_DIM:off + (h + 1) * HEAD_DIM] for h in range(heads)], axis=0)

        def conv_stack(off):
            parts = []
            for h in range(heads):
                cols = slice(off + h * HEAD_DIM, off + (h + 1) * HEAD_DIM)
                parts.append(_conv_slab(cbuf.at[s], 8, toks, cols, convw_ref[:, cols]))
            return jnp.concatenate(parts, axis=0)

        q = _l2norm(conv_stack(0)) * scale
        k = _l2norm(conv_stack(kw))
        v = conv_stack(2 * kw)
        convo_ref[s] = cbuf[s, 8 + toks - 3:8 + toks, :]

        ba = ba_ref[tr, :]
        beta_all = _sigmoid(ba)
        g_all = -jnp.exp(alog_ref[...]) * _softplus(ba + dtb_ref[...])
        beta = jnp.concatenate([beta_all[:, h:h + 1] for h in range(heads)], axis=0)
        gcol = jnp.concatenate([g_all[:, heads + h:heads + h + 1] for h in range(heads)], axis=0)
        e_g = _mm_sel(sel[0:2 * rows], jnp.broadcast_to(gcol, (rows, HEAD_DIM)))
        b_b, suf_b = e_g[0:rows], e_g[rows:2 * rows]
        o, new = _gdn_block(q, k, v, beta, b_b[:, 0:1], b_b.T[0:1, :], suf_b[:, 0:1], lev,
                            [sgi_ref[0, s, h] for h in range(heads)], toks)
        og = _head_norm_gate(o, stack(3 * kw), gnw)
        for h in range(heads):
            sgo_ref[s, h] = new[h]
            mix_ref[tr, h * HEAD_DIM:(h + 1) * HEAD_DIM] = og[h * toks:(h + 1) * toks].astype(mix_ref.dtype)

        base = 4 * kw
        o, new = _hgrn_block(stack(base), stack(base + kw), stack(base + 2 * kw), lb_rows, sel, lev, n_levels,
                             [shi_ref[0, s, h].T for h in range(heads)], toks)
        oh = _head_norm_gate(o, stack(base + 3 * kw), hnw)
        for h in range(heads):
            sho_ref[s, h] = new[h].T
            mix_ref[tr, kw + h * HEAD_DIM:kw + (h + 1) * HEAD_DIM] = (
                oh[h * toks:(h + 1) * toks].astype(mix_ref.dtype))


def _mixer_params(layer, heads, conv_w, a_log, dt_bias, gdn_norm_w, hgrn_norm_w, hgrn_lb):
    pad = HEAD_DIM - 2 * heads
    alog = jnp.pad(a_log[layer][None, :], ((0, 0), (heads, pad)))
    dtb = jnp.pad(dt_bias[layer][None, :], ((0, 0), (heads, pad)))
    return (conv_w[layer], alog, dtb, gdn_norm_w[layer][None, :], hgrn_norm_w[layer][None, :], hgrn_lb)


def _full_spec(a):
    nd = a.ndim
    return pl.BlockSpec(a.shape, lambda *_: (0,) * nd)


def _mix_prompt(proj, ba, params, layer, batch, heads):
    n_chunks = proj.shape[0] // (batch * CHUNK)
    kw = heads * HEAD_DIM
    sel_np, lev_np, n_levels = _block_constants(1, CHUNK)
    sel = jnp.asarray(sel_np, BF16)
    lev = jnp.asarray(lev_np)
    row_map = lambda b, c: (b * n_chunks + c, 0)
    consts = params + (sel, lev)
    kern = functools.partial(_mix_prompt_kernel, heads=heads, layer=layer, n_levels=n_levels)
    return pl.pallas_call(
        kern,
        grid=(batch, n_chunks),
        in_specs=[pl.BlockSpec((CHUNK, 8 * kw), row_map), pl.BlockSpec((CHUNK, HEAD_DIM), row_map)]
                 + [_full_spec(a) for a in consts],
        out_specs=[pl.BlockSpec((CHUNK, 2 * kw), row_map),
                   pl.BlockSpec((1, CONV_W - 1, 3 * kw), lambda b, c: (b, 0, 0)),
                   pl.BlockSpec((1, heads, HEAD_DIM, HEAD_DIM), lambda b, c: (b, 0, 0, 0)),
                   pl.BlockSpec((1, heads, HEAD_DIM, HEAD_DIM), lambda b, c: (b, 0, 0, 0))],
        out_shape=[jax.ShapeDtypeStruct((proj.shape[0], 2 * kw), BF16),
                   jax.ShapeDtypeStruct((batch, CONV_W - 1, 3 * kw), F32),
                   jax.ShapeDtypeStruct((batch, heads, HEAD_DIM, HEAD_DIM), F32),
                   jax.ShapeDtypeStruct((batch, heads, HEAD_DIM, HEAD_DIM), F32)],
        scratch_shapes=[pltpu.VMEM((8 + CHUNK, 3 * kw), F32),
                        pltpu.VMEM((heads, HEAD_DIM, HEAD_DIM), F32),
                        pltpu.VMEM((heads, HEAD_DIM, HEAD_DIM), F32)],
        compiler_params=_cparams(("arbitrary", "arbitrary")),
        name=f"mix_prompt_l{layer}",
    )(proj, ba, *consts)


def _mix_sample(proj, ba, conv_state, gdn_state, hgrn_state, params, layer, n_seq, toks, heads):
    kw = heads * HEAD_DIM
    seqs = SAMPLE_SEQS_PER_STEP
    sel_np, lev_np, n_levels = _block_constants(heads, toks)
    sel = jnp.asarray(sel_np, BF16)
    lev = jnp.asarray(lev_np)
    consts = params + (sel, lev)
    row_map = lambda n: (n, 0)
    kern = functools.partial(_mix_sample_kernel, heads=heads, layer=layer, n_levels=n_levels, toks=toks,
                             seqs=seqs)
    state_in = pl.BlockSpec((1, seqs, heads, HEAD_DIM, HEAD_DIM), lambda n: (layer, n, 0, 0, 0))
    state_out = pl.BlockSpec((seqs, heads, HEAD_DIM, HEAD_DIM), lambda n: (n, 0, 0, 0))
    return pl.pallas_call(
        kern,
        grid=(n_seq // seqs,),
        in_specs=[pl.BlockSpec((seqs * toks, 8 * kw), row_map), pl.BlockSpec((seqs * toks, HEAD_DIM), row_map),
                  pl.BlockSpec((1, seqs, CONV_W - 1, 3 * kw), lambda n: (layer, n, 0, 0)),
                  state_in, state_in] + [_full_spec(a) for a in consts],
        out_specs=[pl.BlockSpec((seqs * toks, 2 * kw), row_map),
                   pl.BlockSpec((seqs, CONV_W - 1, 3 * kw), lambda n: (n, 0, 0)),
                   state_out, state_out],
        out_shape=[jax.ShapeDtypeStruct((n_seq * toks, 2 * kw), BF16),
                   jax.ShapeDtypeStruct((n_seq, CONV_W - 1, 3 * kw), F32),
                   jax.ShapeDtypeStruct((n_seq, heads, HEAD_DIM, HEAD_DIM), F32),
                   jax.ShapeDtypeStruct((n_seq, heads, HEAD_DIM, HEAD_DIM), F32)],
        scratch_shapes=[pltpu.VMEM((seqs, 8 + toks, 3 * kw), F32)],
        compiler_params=_cparams(("arbitrary",)),
        name=f"mix_sample_l{layer}",
    )(proj, ba, conv_state, gdn_state, hgrn_state, *consts)


def _rmsnorm_rows(x, w):
    return x * lax.rsqrt(jnp.mean(x * x, axis=-1, keepdims=True) + EPS) * w


def _norm_kernel(x_ref, w_ref, o_ref):
    o_ref[...] = _rmsnorm_rows(x_ref[...], w_ref[...]).astype(o_ref.dtype)


def _norm(x, w, out_dtype, block_rows, grid, in_map, out_rows, out_map, name):
    d = x.shape[1]
    return pl.pallas_call(
        _norm_kernel,
        grid=grid,
        in_specs=[pl.BlockSpec((block_rows, d), in_map), pl.BlockSpec((1, d), lambda *_: (0, 0))],
        out_specs=pl.BlockSpec((block_rows, d), out_map),
        out_shape=jax.ShapeDtypeStruct((out_rows, d), out_dtype),
        compiler_params=_cparams(("arbitrary",) * len(grid)),
        name=name,
    )(x, w[None, :])


def _proj_kernel(h_ref, w_ref, wba_ref, proj_ref, ba_ref):
    proj_ref[...] = jnp.dot(h_ref[...], w_ref[...], preferred_element_type=F32)

    @pl.when(pl.program_id(1) == 0)
    def _():
        ba_ref[...] = jnp.dot(h_ref[...], wba_ref[...], preferred_element_type=F32)


def _proj(h, w_main, w_ba, name):
    m, d = h.shape
    n = w_main.shape[1]
    tm = _pick_tile(m, 1056)
    tn = _pick_tile(n, 1024, HEAD_DIM)
    return pl.pallas_call(
        _proj_kernel,
        grid=(m // tm, n // tn),
        in_specs=[pl.BlockSpec((tm, d), lambda i, j: (i, 0)),
                  pl.BlockSpec((d, tn), lambda i, j: (0, j)),
                  pl.BlockSpec((d, HEAD_DIM), lambda i, j: (0, 0))],
        out_specs=[pl.BlockSpec((tm, tn), lambda i, j: (i, j)),
                   pl.BlockSpec((tm, HEAD_DIM), lambda i, j: (i, 0))],
        out_shape=[jax.ShapeDtypeStruct((m, n), F32), jax.ShapeDtypeStruct((m, HEAD_DIM), F32)],
        compiler_params=_cparams(("arbitrary", "arbitrary")),
        name=name,
    )(h, w_main, w_ba)


def _wout_kernel(x_ref, mix_ref, w_ref, nw_ref, x1_ref, h_ref):
    x1 = x_ref[...] + jnp.dot(mix_ref[...], w_ref[...], preferred_element_type=F32)
    x1_ref[...] = x1
    h_ref[...] = _rmsnorm_rows(x1, nw_ref[...]).astype(h_ref.dtype)


def _wout(x, mix, w_out, norm_w, name):
    m, d = x.shape
    tm = _pick_tile(m, 352)
    return pl.pallas_call(
        _wout_kernel,
        grid=(m // tm,),
        in_specs=[pl.BlockSpec((tm, d), lambda i: (i, 0)),
                  pl.BlockSpec((tm, mix.shape[1]), lambda i: (i, 0)),
                  pl.BlockSpec(w_out.shape, lambda i: (0, 0)),
                  pl.BlockSpec((1, d), lambda i: (0, 0))],
        out_specs=[pl.BlockSpec((tm, d), lambda i: (i, 0)), pl.BlockSpec((tm, d), lambda i: (i, 0))],
        out_shape=[jax.ShapeDtypeStruct((m, d), F32), jax.ShapeDtypeStruct((m, d), BF16)],
        compiler_params=_cparams(("arbitrary",)),
        name=name,
    )(x, mix, w_out, norm_w[None, :])


def _mlp_kernel(x_ref, h_ref, wup_ref, wdn_ref, nw_ref, xo_ref, ho_ref):
    f = pl.program_id(1)

    @pl.when(f == 0)
    def _():
        xo_ref[...] = x_ref[...]

    u = jnp.maximum(jnp.dot(h_ref[...], wup_ref[...], preferred_element_type=F32), 0.0)
    xo_ref[...] += jnp.dot((u * u).astype(BF16), wdn_ref[...], preferred_element_type=F32)

    @pl.when(f == pl.num_programs(1) - 1)
    def _():
        ho_ref[...] = _rmsnorm_rows(xo_ref[...], nw_ref[...]).astype(ho_ref.dtype)


def _mlp(x, h, w_up, w_down, next_norm_w, name):
    m, d = x.shape
    ff = w_up.shape[1]
    tm = _pick_tile(m, 704)
    tf = _pick_tile(ff, 512, HEAD_DIM)
    return pl.pallas_call(
        _mlp_kernel,
        grid=(m // tm, ff // tf),
        in_specs=[pl.BlockSpec((tm, d), lambda i, f: (i, 0)),
                  pl.BlockSpec((tm, d), lambda i, f: (i, 0)),
                  pl.BlockSpec((d, tf), lambda i, f: (0, f)),
                  pl.BlockSpec((tf, d), lambda i, f: (f, 0)),
                  pl.BlockSpec((1, d), lambda i, f: (0, 0))],
        out_specs=[pl.BlockSpec((tm, d), lambda i, f: (i, 0)), pl.BlockSpec((tm, d), lambda i, f: (i, 0))],
        out_shape=[jax.ShapeDtypeStruct((m, d), F32), jax.ShapeDtypeStruct((m, d), BF16)],
        compiler_params=_cparams(("arbitrary", "arbitrary")),
        name=name,
    )(x, h, w_up, w_down, next_norm_w[None, :])


def kernel(x_prompt, x_sample, state_gdn_conv, state_gdn, state_hgrn, meta_tokens, norm1_w, w_in, conv_w,
           a_log, dt_bias, gdn_norm_w, hgrn_lb, hgrn_norm_w, w_out, norm2_w, w_up, w_down, final_norm_w):
    batch, seq, d = x_prompt.shape
    n_seq, toks, _ = x_sample.shape
    depth = w_in.shape[0]
    heads = a_log.shape[1]
    kw = heads * HEAD_DIM
    n_meta = meta_tokens.shape[0]
    assert seq % CHUNK == 0 and 0 < n_meta <= CHUNK and n_seq % SAMPLE_SEQS_PER_STEP == 0
    assert w_in.shape[2] == 8 * kw + 2 * heads and hgrn_lb.shape[1] == kw and 2 * heads <= HEAD_DIM

    w_main = jnp.concatenate([w_in[:, :, :4 * kw], w_in[:, :, 4 * kw + 2 * heads:]], axis=2).astype(BF16)
    w_ba = jnp.pad(w_in[:, :, 4 * kw:4 * kw + 2 * heads],
                   ((0, 0), (0, 0), (0, HEAD_DIM - 2 * heads))).astype(BF16)
    w_out_b = w_out.astype(BF16)
    w_up_b = w_up.astype(BF16)
    w_down_b = w_down.astype(BF16)

    lead = CHUNK - n_meta
    seq_rows = CHUNK + seq
    xp = jnp.concatenate([jnp.zeros((batch, lead, d), x_prompt.dtype),
                          jnp.broadcast_to(meta_tokens.astype(x_prompt.dtype)[None], (batch, n_meta, d)),
                          x_prompt], axis=1).reshape(batch * seq_rows, d)
    xs = x_sample.reshape(n_seq * toks, d)

    def first_norm(x, name):
        tm = _pick_tile(x.shape[0], 512)
        return _norm(x, norm1_w[0], BF16, tm, (x.shape[0] // tm,), lambda i: (i, 0), x.shape[0],
                     lambda i: (i, 0), name)

    hp = first_norm(xp, "norm_in_prompt")
    hs = first_norm(xs, "norm_in_sample")
    conv_p, gdn_p, hg_p, conv_s, gdn_s, hg_s = [], [], [], [], [], []
    for l in range(depth):
        params = _mixer_params(l, heads, conv_w, a_log, dt_bias, gdn_norm_w, hgrn_norm_w, hgrn_lb)
        next_w = norm1_w[l + 1] if l + 1 < depth else final_norm_w
        proj, ba = _proj(hp, w_main[l], w_ba[l], f"proj_prompt_l{l}")
        mix, c_new, g_new, h_new = _mix_prompt(proj, ba, params, l, batch, heads)
        x1, h2 = _wout(xp, mix, w_out_b[l], norm2_w[l], f"wout_prompt_l{l}")
        xp, hp = _mlp(x1, h2, w_up_b[l], w_down_b[l], next_w, f"mlp_prompt_l{l}")
        conv_p.append(c_new); gdn_p.append(g_new); hg_p.append(h_new)
        proj, ba = _proj(hs, w_main[l], w_ba[l], f"proj_sample_l{l}")
        mix, c_new, g_new, h_new = _mix_sample(proj, ba, state_gdn_conv, state_gdn, state_hgrn, params, l,
                                               n_seq, toks, heads)
        x1, h2 = _wout(xs, mix, w_out_b[l], norm2_w[l], f"wout_sample_l{l}")
        xs, hs = _mlp(x1, h2, w_up_b[l], w_down_b[l], next_w, f"mlp_sample_l{l}")
        conv_s.append(c_new); gdn_s.append(g_new); hg_s.append(h_new)

    blocks = seq // CHUNK
    yp = _norm(xp, final_norm_w, x_prompt.dtype, CHUNK, (batch, blocks),
               lambda b, r: (b * (blocks + 1) + 1 + r, 0), batch * seq,
               lambda b, r: (b * blocks + r, 0), "norm_out_prompt").reshape(batch, seq, d)
    tm = _pick_tile(n_seq * toks, 512)
    ys = _norm(xs, final_norm_w, x_sample.dtype, tm, (n_seq * toks // tm,), lambda i: (i, 0), n_seq * toks,
               lambda i: (i, 0), "norm_out_sample").reshape(n_seq, toks, d)
    return (yp, ys, jnp.stack(conv_p).astype(x_prompt.dtype), jnp.stack(gdn_p).astype(state_gdn.dtype),
            jnp.stack(hg_p).astype(state_hgrn.dtype), jnp.stack(conv_s).astype(state_gdn_conv.dtype),
            jnp.stack(gdn_s).astype(state_gdn.dtype), jnp.stack(hg_s).astype(state_hgrn.dtype))
```

```python
import functools
import math

import numpy as np
import jax
import jax.numpy as jnp
from jax import lax
from jax.experimental import pallas as pl
from jax.experimental.pallas import tpu as pltpu

HEAD_DIM = 128
CHUNK = 64
CONV_W = 4
EPS = 1e-6
SAMPLE_SEQS_PER_STEP = 2
V7X_VMEM_LIMIT_BYTES = 56 * 1024 * 1024

F32 = jnp.float32
BF16 = jnp.bfloat16


def _mm(a, b):
    return jnp.dot(a.astype(BF16), b.astype(BF16), preferred_element_type=F32)


def _mm_nt(a, b):
    return lax.dot_general(a.astype(BF16), b.astype(BF16), (((1,), (1,)), ((), ())),
                           preferred_element_type=F32)


def _mm_tn(a, b):
    return lax.dot_general(a.astype(BF16), b.astype(BF16), (((0,), (0,)), ((), ())),
                           preferred_element_type=F32)


def _mm_sel(m01, x):
    x1 = x.astype(BF16)
    r1 = x - x1.astype(F32)
    x2 = r1.astype(BF16)
    x3 = (r1 - x2.astype(F32)).astype(BF16)
    return (jnp.dot(m01, x1, preferred_element_type=F32)
            + jnp.dot(m01, x2, preferred_element_type=F32)
            + jnp.dot(m01, x3, preferred_element_type=F32))


def _sigmoid(x):
    return 1.0 / (1.0 + jnp.exp(-x))


def _silu(x):
    return x * _sigmoid(x)


def _softplus(x):
    return jnp.maximum(x, 0.0) + jnp.log(1.0 + jnp.exp(-jnp.abs(x)))


def _pick_tile(m, target, mult=16):
    best = None
    for t in range(mult, m + 1, mult):
        if m % t == 0 and t <= target:
            best = t
    if best is None:
        raise ValueError(f"no tile for {m}")
    return best


def _cparams(sem):
    return pltpu.CompilerParams(dimension_semantics=sem, vmem_limit_bytes=V7X_VMEM_LIMIT_BYTES)


@functools.lru_cache(maxsize=None)
def _block_constants(groups, toks):
    rows = groups * toks
    grp = np.arange(rows) // toks
    pos = np.arange(rows) % toks
    same = grp[:, None] == grp[None, :]
    pt, ps = pos[:, None], pos[None, :]
    mats = [same & (ps <= pt), same & (ps > pt)]
    lev = np.full((rows, rows), -1, np.int32)
    lev[np.arange(rows), np.arange(rows)] = 0
    n, idx = toks, 1
    while n >= 2:
        half = n // 2
        blk = same & ((pt // n) == (ps // n))
        qt, qs = pt % n, ps % n
        hi_row = blk & (qt >= half) & (qs >= half) & (qs <= qt)
        lo_row = blk & (qt < half) & (qs > qt) & (qs < half)
        mats.append(hi_row | lo_row)
        lev[blk & (qt >= half) & (qs < half)] = idx
        n //= 2
        idx += 1
    sel = np.concatenate(mats, axis=0).astype(np.float32)
    return sel, lev, idx - 1


def _gdn_blocks(qs, ks, vs, betas, bcols, brows, sufcols, lev, states, toks):
    blocks = range(len(qs))
    rows = qs[0].shape[0]
    groups = range(rows // toks)
    gsl = [slice(g * toks, (g + 1) * toks) for g in groups]
    eye = (lev == 0).astype(F32)
    decay = [jnp.exp(jnp.where(lev >= 0, bcols[p] - brows[p], -jnp.inf)) for p in blocks]
    kb = [ks[p] * betas[p] for p in blocks]
    a = [jnp.where(lev >= 1, _mm_nt(kb[p], ks[p]) * decay[p], 0.0) for p in blocks]
    eb = [jnp.exp(bcols[p]) for p in blocks]
    qe = [qs[p] * eb[p] for p in blocks]
    o_state = [jnp.concatenate([_mm(qe[p][gsl[g]], states[p][g]) for g in groups], axis=0) for p in blocks]
    qk = [_mm_nt(qs[p], ks[p]) * decay[p] for p in blocks]
    inv = [eye - a[p] for p in blocks]
    apow = a
    for _ in range(int(math.log2(toks)) - 1):
        apow = [_mm(x, x) for x in apow]
        inv = [inv[p] + _mm(inv[p], apow[p]) for p in blocks]
    sol = [_mm(inv[p], jnp.concatenate([vs[p] * betas[p], kb[p] * eb[p]], axis=1)) for p in blocks]
    v_new = [sol[p][:, :HEAD_DIM]
             - jnp.concatenate([_mm(sol[p][gsl[g], HEAD_DIM:], states[p][g]) for g in groups], axis=0)
             for p in blocks]
    o = [o_state[p] + _mm(qk[p], v_new[p]) for p in blocks]
    kdec = [ks[p] * jnp.exp(sufcols[p]) for p in blocks]
    new_states = []
    for p in blocks:
        per_group = []
        for g in groups:
            r0 = g * toks
            ebl = jnp.exp(bcols[p][r0:r0 + 1] + sufcols[p][r0:r0 + 1])
            per_group.append(ebl * states[p][g] + _mm_tn(kdec[p][gsl[g]], v_new[p][gsl[g]]))
        new_states.append(per_group)
    return o, new_states


def _hgrn_blocks(qs, f_logits, vs, lbs, sel, lev, n_levels, states_t, toks):
    blocks = range(len(qs))
    rows = qs[0].shape[0]
    groups = range(rows // toks)
    gsl = [slice(g * toks, (g + 1) * toks) for g in groups]
    fg = [lbs[p] + (1.0 - lbs[p]) * _sigmoid(f_logits[p]) for p in blocks]
    ks = [1.0 - fg[p] for p in blocks]
    x = [jnp.exp(_mm_sel(sel, jnp.log(fg[p]))) for p in blocks]
    a = [jnp.where(lev == 0, _mm_nt(qs[p], ks[p]), 0.0) for p in blocks]
    for i in range(n_levels):
        xl = [x[p][(2 + i) * rows:(3 + i) * rows] for p in blocks]
        a = [a[p] + jnp.where(lev == i + 1, _mm_nt(qs[p] * xl[p], ks[p] * xl[p]), 0.0) for p in blocks]
    o = [jnp.concatenate([_mm_nt((qs[p] * x[p][0:rows])[gsl[g]], states_t[p][g]) for g in groups], axis=0)
         + _mm(a[p], vs[p]) for p in blocks]
    new_states = []
    for p in blocks:
        kdec = ks[p] * x[p][rows:2 * rows]
        per_group = []
        for g in groups:
            last = g * toks + toks - 1
            per_group.append(states_t[p][g] * x[p][last:last + 1] + _mm_tn(vs[p][gsl[g]], kdec[gsl[g]]))
        new_states.append(per_group)
    return o, new_states


def _head_norm_gate(o, z, w):
    y = o * lax.rsqrt(jnp.mean(o * o, axis=-1, keepdims=True) + EPS)
    return y * w * _silu(z)


def _l2norm(x):
    return x * lax.rsqrt(jnp.sum(x * x, axis=-1, keepdims=True) + EPS)


def _layer_lower_bound(lbp, layer):
    m = jnp.max(lbp, axis=0, keepdims=True)
    e = jnp.exp(lbp - m)
    p = e / jnp.sum(e, axis=0, keepdims=True)
    if layer == 0:
        return jnp.zeros_like(m)
    return jnp.sum(p[1:layer + 1], axis=0, keepdims=True)


def _conv_slab(cbuf, base, rows, cols, w):
    y = (w[3:4] * cbuf[base:base + rows, cols] + w[2:3] * cbuf[base - 1:base - 1 + rows, cols]
         + w[1:2] * cbuf[base - 2:base - 2 + rows, cols] + w[0:1] * cbuf[base - 3:base - 3 + rows, cols])
    return _silu(y)


def _mix_prompt_kernel(proj_ref, ba_ref, convw_ref, alog_ref, dtb_ref, gnw_ref, hnw_ref, lbp_ref,
                       sel_ref, lev_ref, mix_ref, convo_ref, sgo_ref, sho_ref,
                       cbuf, sg, sht, *, heads, layer, n_levels):
    c = pl.program_id(1)
    kw = heads * HEAD_DIM
    rows = CHUNK

    @pl.when(c == 0)
    def _():
        cbuf[0:8, :] = jnp.zeros((8, 3 * kw), F32)
        sg[...] = jnp.zeros_like(sg)
        sht[...] = jnp.zeros_like(sht)

    cbuf[8:8 + rows, :] = proj_ref[:, 0:3 * kw]
    sel = sel_ref[...]
    lev = lev_ref[...]
    ba = ba_ref[...]
    beta_all = _sigmoid(ba)
    g_all = -jnp.exp(alog_ref[...]) * _softplus(ba + dtb_ref[...])
    e_g = _mm_sel(sel[0:2 * rows], g_all)
    b_all, suf_all = e_g[0:rows], e_g[rows:2 * rows]
    b_all_t = b_all.T
    gnw = gnw_ref[...]
    hnw = hnw_ref[...]
    scale = HEAD_DIM ** -0.5

    hs = range(heads)

    def slab(off, h):
        return slice(off + h * HEAD_DIM, off + (h + 1) * HEAD_DIM)

    def conv(off, h):
        return _conv_slab(cbuf, 8, rows, slab(off, h), convw_ref[:, slab(off, h)])

    qs = [_l2norm(conv(0, h)) * scale for h in hs]
    ks = [_l2norm(conv(kw, h)) for h in hs]
    vs = [conv(2 * kw, h) for h in hs]
    tail = cbuf[8 + rows - 3:8 + rows, :]
    cbuf[5:8, :] = tail
    convo_ref[0] = tail
    os_, new = _gdn_blocks(qs, ks, vs, [beta_all[:, h:h + 1] for h in hs],
                           [b_all[:, heads + h:heads + h + 1] for h in hs],
                           [b_all_t[heads + h:heads + h + 1, :] for h in hs],
                           [suf_all[:, heads + h:heads + h + 1] for h in hs], lev, [[sg[h]] for h in hs], rows)
    for h in hs:
        sg[h] = new[h][0]
        mix_ref[:, slab(0, h)] = _head_norm_gate(os_[h], proj_ref[:, slab(3 * kw, h)], gnw).astype(mix_ref.dtype)

    lb_all = _layer_lower_bound(lbp_ref[...], layer)
    base = 4 * kw
    os_, new = _hgrn_blocks([proj_ref[:, slab(base, h)] for h in hs],
                            [proj_ref[:, slab(base + kw, h)] for h in hs],
                            [proj_ref[:, slab(base + 2 * kw, h)] for h in hs],
                            [lb_all[:, slab(0, h)] for h in hs], sel, lev, n_levels, [[sht[h]] for h in hs], rows)
    for h in hs:
        sht[h] = new[h][0]
        mix_ref[:, slab(kw, h)] = _head_norm_gate(os_[h], proj_ref[:, slab(base + 3 * kw, h)],
                                                  hnw).astype(mix_ref.dtype)

    @pl.when(c == pl.num_programs(1) - 1)
    def _():
        sgo_ref[0] = sg[...]
        for h in range(heads):
            sho_ref[0, h] = sht[h].T


def _mix_sample_kernel(proj_ref, ba_ref, convs_ref, sgi_ref, shi_ref, convw_ref, alog_ref, dtb_ref,
                       gnw_ref, hnw_ref, lbp_ref, sel_ref, lev_ref,
                       mix_ref, convo_ref, sgo_ref, sho_ref, cbuf, *, heads, layer, n_levels, toks, seqs):
    kw = heads * HEAD_DIM
    sel = sel_ref[...]
    lev = lev_ref[...]
    rows = heads * toks
    gnw = gnw_ref[...]
    hnw = hnw_ref[...]
    scale = HEAD_DIM ** -0.5
    lb_all = _layer_lower_bound(lbp_ref[...], layer)
    lb_rows = jnp.concatenate(
        [jnp.broadcast_to(lb_all[:, h * HEAD_DIM:(h + 1) * HEAD_DIM], (toks, HEAD_DIM)) for h in range(heads)],
        axis=0)

    ss = range(seqs)
    hs = range(heads)
    trs = [slice(s * toks, (s + 1) * toks) for s in ss]

    def slab(off, h):
        return slice(off + h * HEAD_DIM, off + (h + 1) * HEAD_DIM)

    def stack(s, off):
        return jnp.concatenate([proj_ref[trs[s], slab(off, h)] for h in hs], axis=0)

    def conv_stack(s, off):
        return jnp.concatenate(
            [_conv_slab(cbuf.at[s], 8, toks, slab(off, h), convw_ref[:, slab(off, h)]) for h in hs], axis=0)

    for s in ss:
        cbuf[s, 5:8, :] = convs_ref[0, s]
        cbuf[s, 8:8 + toks, :] = proj_ref[trs[s], 0:3 * kw]
    qs = [_l2norm(conv_stack(s, 0)) * scale for s in ss]
    ks = [_l2norm(conv_stack(s, kw)) for s in ss]
    vs = [conv_stack(s, 2 * kw) for s in ss]
    for s in ss:
        convo_ref[s] = cbuf[s, 8 + toks - 3:8 + toks, :]

    betas, bcols, brows, sufcols = [], [], [], []
    for s in ss:
        ba = ba_ref[trs[s], :]
        beta_all = _sigmoid(ba)
        g_all = -jnp.exp(alog_ref[...]) * _softplus(ba + dtb_ref[...])
        betas.append(jnp.concatenate([beta_all[:, h:h + 1] for h in hs], axis=0))
        gcol = jnp.concatenate([g_all[:, heads + h:heads + h + 1] for h in hs], axis=0)
        e_g = _mm_sel(sel[0:2 * rows], jnp.broadcast_to(gcol, (rows, HEAD_DIM)))
        bcols.append(e_g[0:rows, 0:1])
        brows.append(e_g[0:rows].T[0:1, :])
        sufcols.append(e_g[rows:2 * rows, 0:1])
    os_, new = _gdn_blocks(qs, ks, vs, betas, bcols, brows, sufcols, lev,
                           [[sgi_ref[0, s, h] for h in hs] for s in ss], toks)
    for s in ss:
        og = _head_norm_gate(os_[s], stack(s, 3 * kw), gnw)
        for h in hs:
            sgo_ref[s, h] = new[s][h]
            mix_ref[trs[s], slab(0, h)] = og[h * toks:(h + 1) * toks].astype(mix_ref.dtype)

    base = 4 * kw
    os_, new = _hgrn_blocks([stack(s, base) for s in ss], [stack(s, base + kw) for s in ss],
                            [stack(s, base + 2 * kw) for s in ss], [lb_rows for s in ss], sel, lev, n_levels,
                            [[shi_ref[0, s, h].T for h in hs] for s in ss], toks)
    for s in ss:
        oh = _head_norm_gate(os_[s], stack(s, base + 3 * kw), hnw)
        for h in hs:
            sho_ref[s, h] = new[s][h].T
            mix_ref[trs[s], slab(kw, h)] = oh[h * toks:(h + 1) * toks].astype(mix_ref.dtype)


def _mixer_params(layer, heads, conv_w, a_log, dt_bias, gdn_norm_w, hgrn_norm_w, hgrn_lb):
    pad = HEAD_DIM - 2 * heads
    alog = jnp.pad(a_log[layer][None, :], ((0, 0), (heads, pad)))
    dtb = jnp.pad(dt_bias[layer][None, :], ((0, 0), (heads, pad)))
    return (conv_w[layer], alog, dtb, gdn_norm_w[layer][None, :], hgrn_norm_w[layer][None, :], hgrn_lb)


def _full_spec(a):
    nd = a.ndim
    return pl.BlockSpec(a.shape, lambda *_: (0,) * nd)


def _mix_prompt(proj, ba, params, layer, batch, heads):
    n_chunks = proj.shape[0] // (batch * CHUNK)
    kw = heads * HEAD_DIM
    sel_np, lev_np, n_levels = _block_constants(1, CHUNK)
    sel = jnp.asarray(sel_np, BF16)
    lev = jnp.asarray(lev_np)
    row_map = lambda b, c: (b * n_chunks + c, 0)
    consts = params + (sel, lev)
    kern = functools.partial(_mix_prompt_kernel, heads=heads, layer=layer, n_levels=n_levels)
    return pl.pallas_call(
        kern,
        grid=(batch, n_chunks),
        in_specs=[pl.BlockSpec((CHUNK, 8 * kw), row_map), pl.BlockSpec((CHUNK, HEAD_DIM), row_map)]
                 + [_full_spec(a) for a in consts],
        out_specs=[pl.BlockSpec((CHUNK, 2 * kw), row_map),
                   pl.BlockSpec((1, CONV_W - 1, 3 * kw), lambda b, c: (b, 0, 0)),
                   pl.BlockSpec((1, heads, HEAD_DIM, HEAD_DIM), lambda b, c: (b, 0, 0, 0)),
                   pl.BlockSpec((1, heads, HEAD_DIM, HEAD_DIM), lambda b, c: (b, 0, 0, 0))],
        out_shape=[jax.ShapeDtypeStruct((proj.shape[0], 2 * kw), BF16),
                   jax.ShapeDtypeStruct((batch, CONV_W - 1, 3 * kw), F32),
                   jax.ShapeDtypeStruct((batch, heads, HEAD_DIM, HEAD_DIM), F32),
                   jax.ShapeDtypeStruct((batch, heads, HEAD_DIM, HEAD_DIM), F32)],
        scratch_shapes=[pltpu.VMEM((8 + CHUNK, 3 * kw), F32),
                        pltpu.VMEM((heads, HEAD_DIM, HEAD_DIM), F32),
                        pltpu.VMEM((heads, HEAD_DIM, HEAD_DIM), F32)],
        compiler_params=_cparams(("arbitrary", "arbitrary")),
        name=f"mix_prompt_l{layer}",
    )(proj, ba, *consts)


def _mix_sample(proj, ba, conv_state, gdn_state, hgrn_state, params, layer, n_seq, toks, heads):
    kw = heads * HEAD_DIM
    seqs = SAMPLE_SEQS_PER_STEP
    sel_np, lev_np, n_levels = _block_constants(heads, toks)
    sel = jnp.asarray(sel_np, BF16)
    lev = jnp.asarray(lev_np)
    consts = params + (sel, lev)
    row_map = lambda n: (n, 0)
    kern = functools.partial(_mix_sample_kernel, heads=heads, layer=layer, n_levels=n_levels, toks=toks,
                             seqs=seqs)
    state_in = pl.BlockSpec((1, seqs, heads, HEAD_DIM, HEAD_DIM), lambda n: (layer, n, 0, 0, 0))
    state_out = pl.BlockSpec((seqs, heads, HEAD_DIM, HEAD_DIM), lambda n: (n, 0, 0, 0))
    return pl.pallas_call(
        kern,
        grid=(n_seq // seqs,),
        in_specs=[pl.BlockSpec((seqs * toks, 8 * kw), row_map), pl.BlockSpec((seqs * toks, HEAD_DIM), row_map),
                  pl.BlockSpec((1, seqs, CONV_W - 1, 3 * kw), lambda n: (layer, n, 0, 0)),
                  state_in, state_in] + [_full_spec(a) for a in consts],
        out_specs=[pl.BlockSpec((seqs * toks, 2 * kw), row_map),
                   pl.BlockSpec((seqs, CONV_W - 1, 3 * kw), lambda n: (n, 0, 0)),
                   state_out, state_out],
        out_shape=[jax.ShapeDtypeStruct((n_seq * toks, 2 * kw), BF16),
                   jax.ShapeDtypeStruct((n_seq, CONV_W - 1, 3 * kw), F32),
                   jax.ShapeDtypeStruct((n_seq, heads, HEAD_DIM, HEAD_DIM), F32),
                   jax.ShapeDtypeStruct((n_seq, heads, HEAD_DIM, HEAD_DIM), F32)],
        scratch_shapes=[pltpu.VMEM((seqs, 8 + toks, 3 * kw), F32)],
        compiler_params=_cparams(("arbitrary",)),
        name=f"mix_sample_l{layer}",
    )(proj, ba, conv_state, gdn_state, hgrn_state, *consts)


def _rmsnorm_rows(x, w):
    return x * lax.rsqrt(jnp.mean(x * x, axis=-1, keepdims=True) + EPS) * w


def _norm_kernel(x_ref, w_ref, o_ref):
    o_ref[...] = _rmsnorm_rows(x_ref[...], w_ref[...]).astype(o_ref.dtype)


def _norm(x, w, out_dtype, block_rows, grid, in_map, out_rows, out_map, name):
    d = x.shape[1]
    return pl.pallas_call(
        _norm_kernel,
        grid=grid,
        in_specs=[pl.BlockSpec((block_rows, d), in_map), pl.BlockSpec((1, d), lambda *_: (0, 0))],
        out_specs=pl.BlockSpec((block_rows, d), out_map),
        out_shape=jax.ShapeDtypeStruct((out_rows, d), out_dtype),
        compiler_params=_cparams(("arbitrary",) * len(grid)),
        name=name,
    )(x, w[None, :])


def _proj_kernel(h_ref, w_ref, wba_ref, proj_ref, ba_ref):
    proj_ref[...] = jnp.dot(h_ref[...], w_ref[...], preferred_element_type=F32)

    @pl.when(pl.program_id(1) == 0)
    def _():
        ba_ref[...] = jnp.dot(h_ref[...], wba_ref[...], preferred_element_type=F32)


def _proj(h, w_main, w_ba, name):
    m, d = h.shape
    n = w_main.shape[1]
    tm = _pick_tile(m, 1056)
    tn = _pick_tile(n, 1024, HEAD_DIM)
    return pl.pallas_call(
        _proj_kernel,
        grid=(m // tm, n // tn),
        in_specs=[pl.BlockSpec((tm, d), lambda i, j: (i, 0)),
                  pl.BlockSpec((d, tn), lambda i, j: (0, j)),
                  pl.BlockSpec((d, HEAD_DIM), lambda i, j: (0, 0))],
        out_specs=[pl.BlockSpec((tm, tn), lambda i, j: (i, j)),
                   pl.BlockSpec((tm, HEAD_DIM), lambda i, j: (i, 0))],
        out_shape=[jax.ShapeDtypeStruct((m, n), F32), jax.ShapeDtypeStruct((m, HEAD_DIM), F32)],
        compiler_params=_cparams(("arbitrary", "arbitrary")),
        name=name,
    )(h, w_main, w_ba)


def _wout_kernel(x_ref, mix_ref, w_ref, nw_ref, x1_ref, h_ref):
    x1 = x_ref[...] + jnp.dot(mix_ref[...], w_ref[...], preferred_element_type=F32)
    x1_ref[...] = x1
    h_ref[...] = _rmsnorm_rows(x1, nw_ref[...]).astype(h_ref.dtype)


def _wout(x, mix, w_out, norm_w, name):
    m, d = x.shape
    tm = _pick_tile(m, 352)
    return pl.pallas_call(
        _wout_kernel,
        grid=(m // tm,),
        in_specs=[pl.BlockSpec((tm, d), lambda i: (i, 0)),
                  pl.BlockSpec((tm, mix.shape[1]), lambda i: (i, 0)),
                  pl.BlockSpec(w_out.shape, lambda i: (0, 0)),
                  pl.BlockSpec((1, d), lambda i: (0, 0))],
        out_specs=[pl.BlockSpec((tm, d), lambda i: (i, 0)), pl.BlockSpec((tm, d), lambda i: (i, 0))],
        out_shape=[jax.ShapeDtypeStruct((m, d), F32), jax.ShapeDtypeStruct((m, d), BF16)],
        compiler_params=_cparams(("arbitrary",)),
        name=name,
    )(x, mix, w_out, norm_w[None, :])


def _mlp_kernel(x_ref, h_ref, wup_ref, wdn_ref, nw_ref, xo_ref, ho_ref):
    f = pl.program_id(1)

    @pl.when(f == 0)
    def _():
        xo_ref[...] = x_ref[...]

    u = jnp.maximum(jnp.dot(h_ref[...], wup_ref[...], preferred_element_type=F32), 0.0)
    xo_ref[...] += jnp.dot((u * u).astype(BF16), wdn_ref[...], preferred_element_type=F32)

    @pl.when(f == pl.num_programs(1) - 1)
    def _():
        ho_ref[...] = _rmsnorm_rows(xo_ref[...], nw_ref[...]).astype(ho_ref.dtype)


def _mlp(x, h, w_up, w_down, next_norm_w, name):
    m, d = x.shape
    ff = w_up.shape[1]
    tm = _pick_tile(m, 704)
    tf = _pick_tile(ff, 512, HEAD_DIM)
    return pl.pallas_call(
        _mlp_kernel,
        grid=(m // tm, ff // tf),
        in_specs=[pl.BlockSpec((tm, d), lambda i, f: (i, 0)),
                  pl.BlockSpec((tm, d), lambda i, f: (i, 0)),
                  pl.BlockSpec((d, tf), lambda i, f: (0, f)),
                  pl.BlockSpec((tf, d), lambda i, f: (f, 0)),
                  pl.BlockSpec((1, d), lambda i, f: (0, 0))],
        out_specs=[pl.BlockSpec((tm, d), lambda i, f: (i, 0)), pl.BlockSpec((tm, d), lambda i, f: (i, 0))],
        out_shape=[jax.ShapeDtypeStruct((m, d), F32), jax.ShapeDtypeStruct((m, d), BF16)],
        compiler_params=_cparams(("arbitrary", "arbitrary")),
        name=name,
    )(x, h, w_up, w_down, next_norm_w[None, :])


def kernel(x_prompt, x_sample, state_gdn_conv, state_gdn, state_hgrn, meta_tokens, norm1_w, w_in, conv_w,
           a_log, dt_bias, gdn_norm_w, hgrn_lb, hgrn_norm_w, w_out, norm2_w, w_up, w_down, final_norm_w):
    batch, seq, d = x_prompt.shape
    n_seq, toks, _ = x_sample.shape
    depth = w_in.shape[0]
    heads = a_log.shape[1]
    kw = heads * HEAD_DIM
    n_meta = meta_tokens.shape[0]
    assert seq % CHUNK == 0 and 0 < n_meta <= CHUNK and n_seq % SAMPLE_SEQS_PER_STEP == 0
    assert w_in.shape[2] == 8 * kw + 2 * heads and hgrn_lb.shape[1] == kw and 2 * heads <= HEAD_DIM

    w_main = jnp.concatenate([w_in[:, :, :4 * kw], w_in[:, :, 4 * kw + 2 * heads:]], axis=2).astype(BF16)
    w_ba = jnp.pad(w_in[:, :, 4 * kw:4 * kw + 2 * heads],
                   ((0, 0), (0, 0), (0, HEAD_DIM - 2 * heads))).astype(BF16)
    w_out_b = w_out.astype(BF16)
    w_up_b = w_up.astype(BF16)
    w_down_b = w_down.astype(BF16)

    lead = CHUNK - n_meta
    seq_rows = CHUNK + seq
    xp = jnp.concatenate([jnp.zeros((batch, lead, d), x_prompt.dtype),
                          jnp.broadcast_to(meta_tokens.astype(x_prompt.dtype)[None], (batch, n_meta, d)),
                          x_prompt], axis=1).reshape(batch * seq_rows, d)
    xs = x_sample.reshape(n_seq * toks, d)

    def first_norm(x, name):
        tm = _pick_tile(x.shape[0], 512)
        return _norm(x, norm1_w[0], BF16, tm, (x.shape[0] // tm,), lambda i: (i, 0), x.shape[0],
                     lambda i: (i, 0), name)

    hp = first_norm(xp, "norm_in_prompt")
    hs = first_norm(xs, "norm_in_sample")
    conv_p, gdn_p, hg_p, conv_s, gdn_s, hg_s = [], [], [], [], [], []
    for l in range(depth):
        params = _mixer_params(l, heads, conv_w, a_log, dt_bias, gdn_norm_w, hgrn_norm_w, hgrn_lb)
        next_w = norm1_w[l + 1] if l + 1 < depth else final_norm_w
        proj, ba = _proj(hp, w_main[l], w_ba[l], f"proj_prompt_l{l}")
        mix, c_new, g_new, h_new = _mix_prompt(proj, ba, params, l, batch, heads)
        x1, h2 = _wout(xp, mix, w_out_b[l], norm2_w[l], f"wout_prompt_l{l}")
        xp, hp = _mlp(x1, h2, w_up_b[l], w_down_b[l], next_w, f"mlp_prompt_l{l}")
        conv_p.append(c_new); gdn_p.append(g_new); hg_p.append(h_new)
        proj, ba = _proj(hs, w_main[l], w_ba[l], f"proj_sample_l{l}")
        mix, c_new, g_new, h_new = _mix_sample(proj, ba, state_gdn_conv, state_gdn, state_hgrn, params, l,
                                               n_seq, toks, heads)
        x1, h2 = _wout(xs, mix, w_out_b[l], norm2_w[l], f"wout_sample_l{l}")
        xs, hs = _mlp(x1, h2, w_up_b[l], w_down_b[l], next_w, f"mlp_sample_l{l}")
        conv_s.append(c_new); gdn_s.append(g_new); hg_s.append(h_new)

    blocks = seq // CHUNK
    yp = _norm(xp, final_norm_w, x_prompt.dtype, CHUNK, (batch, blocks),
               lambda b, r: (b * (blocks + 1) + 1 + r, 0), batch * seq,
               lambda b, r: (b * blocks + r, 0), "norm_out_prompt").reshape(batch, seq, d)
    tm = _pick_tile(n_seq * toks, 512)
    ys = _norm(xs, final_norm_w, x_sample.dtype, tm, (n_seq * toks // tm,), lambda i: (i, 0), n_seq * toks,
               lambda i: (i, 0), "norm_out_sample").reshape(n_seq, toks, d)
    return (yp, ys, jnp.stack(conv_p).astype(x_prompt.dtype), jnp.stack(gdn_p).astype(state_gdn.dtype),
            jnp.stack(hg_p).astype(state_hgrn.dtype), jnp.stack(conv_s).astype(state_gdn_conv.dtype),
            jnp.stack(gdn_s).astype(state_gdn.dtype), jnp.stack(hg_s).astype(state_hgrn.dtype))
```

```python
import functools
import math

import numpy as np
import jax
import jax.numpy as jnp
from jax import lax
from jax.experimental import pallas as pl
from jax.experimental.pallas import tpu as pltpu

HEAD_DIM = 128
CHUNK = 64
CONV_W = 4
EPS = 1e-6
SAMPLE_SEQS_PER_STEP = 2
V7X_VMEM_LIMIT_BYTES = 56 * 1024 * 1024

F32 = jnp.float32
BF16 = jnp.bfloat16


def _mm(a, b):
    return jnp.dot(a.astype(BF16), b.astype(BF16), preferred_element_type=F32)


def _mm_nt(a, b):
    return lax.dot_general(a.astype(BF16), b.astype(BF16), (((1,), (1,)), ((), ())),
                           preferred_element_type=F32)


def _mm_tn(a, b):
    return lax.dot_general(a.astype(BF16), b.astype(BF16), (((0,), (0,)), ((), ())),
                           preferred_element_type=F32)


def _split_bf16(x):
    hi = x.astype(BF16)
    return hi, (x - hi.astype(F32)).astype(BF16)


def _mm_3pass(a, b):
    ah, al = _split_bf16(a)
    bh, bl = _split_bf16(b)
    return (jnp.dot(ah, bh, preferred_element_type=F32) + jnp.dot(ah, bl, preferred_element_type=F32)
            + jnp.dot(al, bh, preferred_element_type=F32))


def _mm_sel(m01, x):
    x1 = x.astype(BF16)
    r1 = x - x1.astype(F32)
    x2 = r1.astype(BF16)
    x3 = (r1 - x2.astype(F32)).astype(BF16)
    return (jnp.dot(m01, x1, preferred_element_type=F32)
            + jnp.dot(m01, x2, preferred_element_type=F32)
            + jnp.dot(m01, x3, preferred_element_type=F32))


def _sigmoid(x):
    return 1.0 / (1.0 + jnp.exp(-x))


def _silu(x):
    return x * _sigmoid(x)


def _softplus(x):
    return jnp.maximum(x, 0.0) + jnp.log(1.0 + jnp.exp(-jnp.abs(x)))


def _pick_tile(m, target, mult=16):
    best = None
    for t in range(mult, m + 1, mult):
        if m % t == 0 and t <= target:
            best = t
    if best is None:
        raise ValueError(f"no tile for {m}")
    return best


def _cparams(sem):
    return pltpu.CompilerParams(dimension_semantics=sem, vmem_limit_bytes=V7X_VMEM_LIMIT_BYTES)


@functools.lru_cache(maxsize=None)
def _block_constants(groups, toks):
    rows = groups * toks
    grp = np.arange(rows) // toks
    pos = np.arange(rows) % toks
    same = grp[:, None] == grp[None, :]
    pt, ps = pos[:, None], pos[None, :]
    mats = [same & (ps <= pt), same & (ps > pt)]
    lev = np.full((rows, rows), -1, np.int32)
    lev[np.arange(rows), np.arange(rows)] = 0
    n, idx = toks, 1
    while n >= 2:
        half = n // 2
        blk = same & ((pt // n) == (ps // n))
        qt, qs = pt % n, ps % n
        hi_row = blk & (qt >= half) & (qs >= half) & (qs <= qt)
        lo_row = blk & (qt < half) & (qs > qt) & (qs < half)
        mats.append(hi_row | lo_row)
        lev[blk & (qt >= half) & (qs < half)] = idx
        n //= 2
        idx += 1
    sel = np.concatenate(mats, axis=0).astype(np.float32)
    return sel, lev, idx - 1


def _gdn_blocks(qs, ks, vs, betas, bcols, brows, sufcols, lev, states, toks):
    blocks = range(len(qs))
    rows = qs[0].shape[0]
    groups = range(rows // toks)
    gsl = [slice(g * toks, (g + 1) * toks) for g in groups]
    eye = (lev == 0).astype(F32)
    decay = [jnp.exp(jnp.where(lev >= 0, bcols[p] - brows[p], -jnp.inf)) for p in blocks]
    kb = [ks[p] * betas[p] for p in blocks]
    a = [jnp.where(lev >= 1, _mm_nt(kb[p], ks[p]) * decay[p], 0.0) for p in blocks]
    eb = [jnp.exp(bcols[p]) for p in blocks]
    qe = [qs[p] * eb[p] for p in blocks]
    o_state = [jnp.concatenate([_mm(qe[p][gsl[g]], states[p][g]) for g in groups], axis=0) for p in blocks]
    qk = [_mm_nt(qs[p], ks[p]) * decay[p] for p in blocks]
    inv = [eye - a[p] for p in blocks]
    apow = a
    for _ in range(int(math.log2(toks)) - 1):
        apow = [_mm_3pass(x, x) for x in apow]
        inv = [inv[p] + _mm_3pass(inv[p], apow[p]) for p in blocks]
    sol = [_mm_3pass(inv[p], jnp.concatenate([vs[p] * betas[p], kb[p] * eb[p]], axis=1)) for p in blocks]
    v_new = [sol[p][:, :HEAD_DIM]
             - jnp.concatenate([_mm(sol[p][gsl[g], HEAD_DIM:], states[p][g]) for g in groups], axis=0)
             for p in blocks]
    o = [o_state[p] + _mm(qk[p], v_new[p]) for p in blocks]
    kdec = [ks[p] * jnp.exp(sufcols[p]) for p in blocks]
    new_states = []
    for p in blocks:
        per_group = []
        for g in groups:
            r0 = g * toks
            ebl = jnp.exp(bcols[p][r0:r0 + 1] + sufcols[p][r0:r0 + 1])
            per_group.append(ebl * states[p][g] + _mm_tn(kdec[p][gsl[g]], v_new[p][gsl[g]]))
        new_states.append(per_group)
    return o, new_states


def _hgrn_blocks(qs, f_logits, vs, lbs, sel, lev, n_levels, states_t, toks):
    blocks = range(len(qs))
    rows = qs[0].shape[0]
    groups = range(rows // toks)
    gsl = [slice(g * toks, (g + 1) * toks) for g in groups]
    fg = [lbs[p] + (1.0 - lbs[p]) * _sigmoid(f_logits[p]) for p in blocks]
    ks = [1.0 - fg[p] for p in blocks]
    x = [jnp.exp(_mm_sel(sel, jnp.log(fg[p]))) for p in blocks]
    a = [jnp.where(lev == 0, _mm_nt(qs[p], ks[p]), 0.0) for p in blocks]
    for i in range(n_levels):
        xl = [x[p][(2 + i) * rows:(3 + i) * rows] for p in blocks]
        a = [a[p] + jnp.where(lev == i + 1, _mm_nt(qs[p] * xl[p], ks[p] * xl[p]), 0.0) for p in blocks]
    o = [jnp.concatenate([_mm_nt((qs[p] * x[p][0:rows])[gsl[g]], states_t[p][g]) for g in groups], axis=0)
         + _mm(a[p], vs[p]) for p in blocks]
    new_states = []
    for p in blocks:
        kdec = ks[p] * x[p][rows:2 * rows]
        per_group = []
        for g in groups:
            last = g * toks + toks - 1
            per_group.append(states_t[p][g] * x[p][last:last + 1] + _mm_tn(vs[p][gsl[g]], kdec[gsl[g]]))
        new_states.append(per_group)
    return o, new_states


def _head_norm_gate(o, z, w):
    y = o * lax.rsqrt(jnp.mean(o * o, axis=-1, keepdims=True) + EPS)
    return y * w * _silu(z)


def _l2norm(x):
    return x * lax.rsqrt(jnp.sum(x * x, axis=-1, keepdims=True) + EPS)


def _layer_lower_bound(lbp, layer):
    m = jnp.max(lbp, axis=0, keepdims=True)
    e = jnp.exp(lbp - m)
    p = e / jnp.sum(e, axis=0, keepdims=True)
    if layer == 0:
        return jnp.zeros_like(m)
    return jnp.sum(p[1:layer + 1], axis=0, keepdims=True)


def _conv_slab(cbuf, base, rows, cols, w):
    y = (w[3:4] * cbuf[base:base + rows, cols] + w[2:3] * cbuf[base - 1:base - 1 + rows, cols]
         + w[1:2] * cbuf[base - 2:base - 2 + rows, cols] + w[0:1] * cbuf[base - 3:base - 3 + rows, cols])
    return _silu(y)


def _mix_prompt_kernel(proj_ref, ba_ref, convw_ref, alog_ref, dtb_ref, gnw_ref, hnw_ref, lbp_ref,
                       sel_ref, lev_ref, mix_ref, convo_ref, sgo_ref, sho_ref,
                       cbuf, sg, sht, *, heads, layer, n_levels):
    c = pl.program_id(1)
    kw = heads * HEAD_DIM
    rows = CHUNK

    @pl.when(c == 0)
    def _():
        cbuf[0:8, :] = jnp.zeros((8, 3 * kw), F32)
        sg[...] = jnp.zeros_like(sg)
        sht[...] = jnp.zeros_like(sht)

    cbuf[8:8 + rows, :] = proj_ref[:, 0:3 * kw]
    sel = sel_ref[...]
    lev = lev_ref[...]
    ba = ba_ref[...]
    beta_all = _sigmoid(ba)
    g_all = -jnp.exp(alog_ref[...]) * _softplus(ba + dtb_ref[...])
    e_g = _mm_sel(sel[0:2 * rows], g_all)
    b_all, suf_all = e_g[0:rows], e_g[rows:2 * rows]
    b_all_t = b_all.T
    gnw = gnw_ref[...]
    hnw = hnw_ref[...]
    scale = HEAD_DIM ** -0.5

    hs = range(heads)

    def slab(off, h):
        return slice(off + h * HEAD_DIM, off + (h + 1) * HEAD_DIM)

    def conv(off, h):
        return _conv_slab(cbuf, 8, rows, slab(off, h), convw_ref[:, slab(off, h)])

    qs = [_l2norm(conv(0, h)) * scale for h in hs]
    ks = [_l2norm(conv(kw, h)) for h in hs]
    vs = [conv(2 * kw, h) for h in hs]
    tail = cbuf[8 + rows - 3:8 + rows, :]
    cbuf[5:8, :] = tail
    convo_ref[0] = tail
    os_, new = _gdn_blocks(qs, ks, vs, [beta_all[:, h:h + 1] for h in hs],
                           [b_all[:, heads + h:heads + h + 1] for h in hs],
                           [b_all_t[heads + h:heads + h + 1, :] for h in hs],
                           [suf_all[:, heads + h:heads + h + 1] for h in hs], lev, [[sg[h]] for h in hs], rows)
    for h in hs:
        sg[h] = new[h][0]
        mix_ref[:, slab(0, h)] = _head_norm_gate(os_[h], proj_ref[:, slab(3 * kw, h)], gnw).astype(mix_ref.dtype)

    lb_all = _layer_lower_bound(lbp_ref[...], layer)
    base = 4 * kw
    os_, new = _hgrn_blocks([proj_ref[:, slab(base, h)] for h in hs],
                            [proj_ref[:, slab(base + kw, h)] for h in hs],
                            [proj_ref[:, slab(base + 2 * kw, h)] for h in hs],
                            [lb_all[:, slab(0, h)] for h in hs], sel, lev, n_levels, [[sht[h]] for h in hs], rows)
    for h in hs:
        sht[h] = new[h][0]
        mix_ref[:, slab(kw, h)] = _head_norm_gate(os_[h], proj_ref[:, slab(base + 3 * kw, h)],
                                                  hnw).astype(mix_ref.dtype)

    @pl.when(c == pl.num_programs(1) - 1)
    def _():
        sgo_ref[0] = sg[...]
        for h in range(heads):
            sho_ref[0, h] = sht[h].T


def _mix_sample_kernel(proj_ref, ba_ref, convs_ref, sgi_ref, shi_ref, convw_ref, alog_ref, dtb_ref,
                       gnw_ref, hnw_ref, lbp_ref, sel_ref, lev_ref, *rest,
                       heads, layer, n_levels, toks, seqs):
    mix_ref, convo_ref, sgo_ref, sho_ref, cbuf = rest[-5:]
    kw = heads * HEAD_DIM
    sel = sel_ref[...]
    lev = lev_ref[...]
    rows = heads * toks
    gnw = gnw_ref[...]
    hnw = hnw_ref[...]
    scale = HEAD_DIM ** -0.5
    lb_all = _layer_lower_bound(lbp_ref[...], layer)
    lb_rows = jnp.concatenate(
        [jnp.broadcast_to(lb_all[:, h * HEAD_DIM:(h + 1) * HEAD_DIM], (toks, HEAD_DIM)) for h in range(heads)],
        axis=0)

    ss = range(seqs)
    hs = range(heads)
    trs = [slice(s * toks, (s + 1) * toks) for s in ss]

    def slab(off, h):
        return slice(off + h * HEAD_DIM, off + (h + 1) * HEAD_DIM)

    def stack(s, off):
        return jnp.concatenate([proj_ref[trs[s], slab(off, h)] for h in hs], axis=0)

    def conv_stack(s, off):
        return jnp.concatenate(
            [_conv_slab(cbuf.at[s], 8, toks, slab(off, h), convw_ref[:, slab(off, h)]) for h in hs], axis=0)

    for s in ss:
        cbuf[s, 5:8, :] = convs_ref[0, s]
        cbuf[s, 8:8 + toks, :] = proj_ref[trs[s], 0:3 * kw]
    qs = [_l2norm(conv_stack(s, 0)) * scale for s in ss]
    ks = [_l2norm(conv_stack(s, kw)) for s in ss]
    vs = [conv_stack(s, 2 * kw) for s in ss]
    for s in ss:
        convo_ref[s] = cbuf[s, 8 + toks - 3:8 + toks, :]

    betas, bcols, brows, sufcols = [], [], [], []
    for s in ss:
        ba = ba_ref[trs[s], :]
        beta_all = _sigmoid(ba)
        g_all = -jnp.exp(alog_ref[...]) * _softplus(ba + dtb_ref[...])
        betas.append(jnp.concatenate([beta_all[:, h:h + 1] for h in hs], axis=0))
        gcol = jnp.concatenate([g_all[:, heads + h:heads + h + 1] for h in hs], axis=0)
        e_g = _mm_sel(sel[0:2 * rows], jnp.broadcast_to(gcol, (rows, HEAD_DIM)))
        bcols.append(e_g[0:rows, 0:1])
        brows.append(e_g[0:rows].T[0:1, :])
        sufcols.append(e_g[rows:2 * rows, 0:1])
    os_, new = _gdn_blocks(qs, ks, vs, betas, bcols, brows, sufcols, lev,
                           [[sgi_ref[0, s, h] for h in hs] for s in ss], toks)
    for s in ss:
        og = _head_norm_gate(os_[s], stack(s, 3 * kw), gnw)
        for h in hs:
            sgo_ref[0, s, h] = new[s][h]
            mix_ref[trs[s], slab(0, h)] = og[h * toks:(h + 1) * toks].astype(mix_ref.dtype)

    base = 4 * kw
    os_, new = _hgrn_blocks([stack(s, base) for s in ss], [stack(s, base + kw) for s in ss],
                            [stack(s, base + 2 * kw) for s in ss], [lb_rows for s in ss], sel, lev, n_levels,
                            [[shi_ref[0, s, h].T for h in hs] for s in ss], toks)
    for s in ss:
        oh = _head_norm_gate(os_[s], stack(s, base + 3 * kw), hnw)
        for h in hs:
            sho_ref[0, s, h] = new[s][h].T
            mix_ref[trs[s], slab(kw, h)] = oh[h * toks:(h + 1) * toks].astype(mix_ref.dtype)


def _mixer_params(layer, heads, conv_w, a_log, dt_bias, gdn_norm_w, hgrn_norm_w, hgrn_lb):
    pad = HEAD_DIM - 2 * heads
    alog = jnp.pad(a_log[layer][None, :], ((0, 0), (heads, pad)))
    dtb = jnp.pad(dt_bias[layer][None, :], ((0, 0), (heads, pad)))
    return (conv_w[layer], alog, dtb, gdn_norm_w[layer][None, :], hgrn_norm_w[layer][None, :], hgrn_lb)


def _full_spec(a):
    nd = a.ndim
    return pl.BlockSpec(a.shape, lambda *_: (0,) * nd)


def _mix_prompt(proj, ba, params, layer, batch, heads):
    n_chunks = proj.shape[0] // (batch * CHUNK)
    kw = heads * HEAD_DIM
    sel_np, lev_np, n_levels = _block_constants(1, CHUNK)
    sel = jnp.asarray(sel_np, BF16)
    lev = jnp.asarray(lev_np)
    row_map = lambda b, c: (b * n_chunks + c, 0)
    consts = params + (sel, lev)
    kern = functools.partial(_mix_prompt_kernel, heads=heads, layer=layer, n_levels=n_levels)
    return pl.pallas_call(
        kern,
        grid=(batch, n_chunks),
        in_specs=[pl.BlockSpec((CHUNK, 8 * kw), row_map), pl.BlockSpec((CHUNK, HEAD_DIM), row_map)]
                 + [_full_spec(a) for a in consts],
        out_specs=[pl.BlockSpec((CHUNK, 2 * kw), row_map),
                   pl.BlockSpec((1, CONV_W - 1, 3 * kw), lambda b, c: (b, 0, 0)),
                   pl.BlockSpec((1, heads, HEAD_DIM, HEAD_DIM), lambda b, c: (b, 0, 0, 0)),
                   pl.BlockSpec((1, heads, HEAD_DIM, HEAD_DIM), lambda b, c: (b, 0, 0, 0))],
        out_shape=[jax.ShapeDtypeStruct((proj.shape[0], 2 * kw), BF16),
                   jax.ShapeDtypeStruct((batch, CONV_W - 1, 3 * kw), F32),
                   jax.ShapeDtypeStruct((batch, heads, HEAD_DIM, HEAD_DIM), F32),
                   jax.ShapeDtypeStruct((batch, heads, HEAD_DIM, HEAD_DIM), F32)],
        scratch_shapes=[pltpu.VMEM((8 + CHUNK, 3 * kw), F32),
                        pltpu.VMEM((heads, HEAD_DIM, HEAD_DIM), F32),
                        pltpu.VMEM((heads, HEAD_DIM, HEAD_DIM), F32)],
        compiler_params=_cparams(("arbitrary", "arbitrary")),
        name=f"mix_prompt_l{layer}",
    )(proj, ba, *consts)


def _mix_sample(proj, ba, conv_state, gdn_state, hgrn_state, params, layer, n_seq, toks, heads, carried):
    kw = heads * HEAD_DIM
    seqs = SAMPLE_SEQS_PER_STEP
    sel_np, lev_np, n_levels = _block_constants(heads, toks)
    sel = jnp.asarray(sel_np, BF16)
    lev = jnp.asarray(lev_np)
    consts = params + (sel, lev)
    row_map = lambda n: (n, 0)
    kern = functools.partial(_mix_sample_kernel, heads=heads, layer=layer, n_levels=n_levels, toks=toks,
                             seqs=seqs)
    state_blk = pl.BlockSpec((1, seqs, heads, HEAD_DIM, HEAD_DIM), lambda n: (layer, n, 0, 0, 0))
    in_specs = [pl.BlockSpec((seqs * toks, 8 * kw), row_map), pl.BlockSpec((seqs * toks, HEAD_DIM), row_map),
                pl.BlockSpec((1, seqs, CONV_W - 1, 3 * kw), lambda n: (layer, n, 0, 0)),
                state_blk, state_blk] + [_full_spec(a) for a in consts]
    args = (proj, ba, conv_state, gdn_state, hgrn_state) + consts
    aliases = {}
    if carried is not None:
        aliases = {len(args): 2, len(args) + 1: 3}
        in_specs += [pl.BlockSpec(memory_space=pl.ANY), pl.BlockSpec(memory_space=pl.ANY)]
        args += tuple(carried)
    return pl.pallas_call(
        kern,
        grid=(n_seq // seqs,),
        in_specs=in_specs,
        out_specs=[pl.BlockSpec((seqs * toks, 2 * kw), row_map),
                   pl.BlockSpec((seqs, CONV_W - 1, 3 * kw), lambda n: (n, 0, 0)),
                   state_blk, state_blk],
        out_shape=[jax.ShapeDtypeStruct((n_seq * toks, 2 * kw), BF16),
                   jax.ShapeDtypeStruct((n_seq, CONV_W - 1, 3 * kw), F32),
                   jax.ShapeDtypeStruct(gdn_state.shape, F32),
                   jax.ShapeDtypeStruct(hgrn_state.shape, F32)],
        scratch_shapes=[pltpu.VMEM((seqs, 8 + toks, 3 * kw), F32)],
        input_output_aliases=aliases,
        compiler_params=_cparams(("arbitrary",)),
        name=f"mix_sample_l{layer}",
    )(*args)


def _rmsnorm_rows(x, w):
    return x * lax.rsqrt(jnp.mean(x * x, axis=-1, keepdims=True) + EPS) * w


def _norm_kernel(x_ref, w_ref, o_ref):
    o_ref[...] = _rmsnorm_rows(x_ref[...], w_ref[...]).astype(o_ref.dtype)


def _norm(x, w, out_dtype, block_rows, grid, in_map, out_rows, out_map, name):
    d = x.shape[1]
    return pl.pallas_call(
        _norm_kernel,
        grid=grid,
        in_specs=[pl.BlockSpec((block_rows, d), in_map), pl.BlockSpec((1, d), lambda *_: (0, 0))],
        out_specs=pl.BlockSpec((block_rows, d), out_map),
        out_shape=jax.ShapeDtypeStruct((out_rows, d), out_dtype),
        compiler_params=_cparams(("arbitrary",) * len(grid)),
        name=name,
    )(x, w[None, :])


def _proj_kernel(h_ref, w_ref, wba_ref, proj_ref, ba_ref):
    proj_ref[...] = jnp.dot(h_ref[...], w_ref[...], preferred_element_type=F32)

    @pl.when(pl.program_id(1) == 0)
    def _():
        ba_ref[...] = jnp.dot(h_ref[...], wba_ref[...], preferred_element_type=F32)


def _proj(h, w_main, w_ba, name):
    m, d = h.shape
    n = w_main.shape[1]
    tm = _pick_tile(m, 1056)
    tn = _pick_tile(n, 1024, HEAD_DIM)
    return pl.pallas_call(
        _proj_kernel,
        grid=(m // tm, n // tn),
        in_specs=[pl.BlockSpec((tm, d), lambda i, j: (i, 0)),
                  pl.BlockSpec((d, tn), lambda i, j: (0, j)),
                  pl.BlockSpec((d, HEAD_DIM), lambda i, j: (0, 0))],
        out_specs=[pl.BlockSpec((tm, tn), lambda i, j: (i, j)),
                   pl.BlockSpec((tm, HEAD_DIM), lambda i, j: (i, 0))],
        out_shape=[jax.ShapeDtypeStruct((m, n), F32), jax.ShapeDtypeStruct((m, HEAD_DIM), F32)],
        compiler_params=_cparams(("arbitrary", "arbitrary")),
        name=name,
    )(h, w_main, w_ba)


def _wout_kernel(x_ref, mix_ref, w_ref, nw_ref, x1_ref, h_ref):
    x1 = x_ref[...] + jnp.dot(mix_ref[...], w_ref[...], preferred_element_type=F32)
    x1_ref[...] = x1
    h_ref[...] = _rmsnorm_rows(x1, nw_ref[...]).astype(h_ref.dtype)


def _wout(x, mix, w_out, norm_w, name):
    m, d = x.shape
    tm = _pick_tile(m, 352)
    return pl.pallas_call(
        _wout_kernel,
        grid=(m // tm,),
        in_specs=[pl.BlockSpec((tm, d), lambda i: (i, 0)),
                  pl.BlockSpec((tm, mix.shape[1]), lambda i: (i, 0)),
                  pl.BlockSpec(w_out.shape, lambda i: (0, 0)),
                  pl.BlockSpec((1, d), lambda i: (0, 0))],
        out_specs=[pl.BlockSpec((tm, d), lambda i: (i, 0)), pl.BlockSpec((tm, d), lambda i: (i, 0))],
        out_shape=[jax.ShapeDtypeStruct((m, d), F32), jax.ShapeDtypeStruct((m, d), BF16)],
        compiler_params=_cparams(("arbitrary",)),
        name=name,
    )(x, mix, w_out, norm_w[None, :])


def _mlp_kernel(x_ref, h_ref, wup_ref, wdn_ref, nw_ref, xo_ref, ho_ref):
    f = pl.program_id(1)

    @pl.when(f == 0)
    def _():
        xo_ref[...] = x_ref[...]

    u = jnp.maximum(jnp.dot(h_ref[...], wup_ref[...], preferred_element_type=F32), 0.0)
    xo_ref[...] += jnp.dot((u * u).astype(BF16), wdn_ref[...], preferred_element_type=F32)

    @pl.when(f == pl.num_programs(1) - 1)
    def _():
        ho_ref[...] = _rmsnorm_rows(xo_ref[...], nw_ref[...]).astype(ho_ref.dtype)


def _mlp(x, h, w_up, w_down, next_norm_w, name):
    m, d = x.shape
    ff = w_up.shape[1]
    tm = _pick_tile(m, 704)
    tf = _pick_tile(ff, 512, HEAD_DIM)
    return pl.pallas_call(
        _mlp_kernel,
        grid=(m // tm, ff // tf),
        in_specs=[pl.BlockSpec((tm, d), lambda i, f: (i, 0)),
                  pl.BlockSpec((tm, d), lambda i, f: (i, 0)),
                  pl.BlockSpec((d, tf), lambda i, f: (0, f)),
                  pl.BlockSpec((tf, d), lambda i, f: (f, 0)),
                  pl.BlockSpec((1, d), lambda i, f: (0, 0))],
        out_specs=[pl.BlockSpec((tm, d), lambda i, f: (i, 0)), pl.BlockSpec((tm, d), lambda i, f: (i, 0))],
        out_shape=[jax.ShapeDtypeStruct((m, d), F32), jax.ShapeDtypeStruct((m, d), BF16)],
        compiler_params=_cparams(("arbitrary", "arbitrary")),
        name=name,
    )(x, h, w_up, w_down, next_norm_w[None, :])


def kernel(x_prompt, x_sample, state_gdn_conv, state_gdn, state_hgrn, meta_tokens, norm1_w, w_in, conv_w,
           a_log, dt_bias, gdn_norm_w, hgrn_lb, hgrn_norm_w, w_out, norm2_w, w_up, w_down, final_norm_w):
    batch, seq, d = x_prompt.shape
    n_seq, toks, _ = x_sample.shape
    depth = w_in.shape[0]
    heads = a_log.shape[1]
    kw = heads * HEAD_DIM
    n_meta = meta_tokens.shape[0]
    assert seq % CHUNK == 0 and 0 < n_meta <= CHUNK and n_seq % SAMPLE_SEQS_PER_STEP == 0
    assert w_in.shape[2] == 8 * kw + 2 * heads and hgrn_lb.shape[1] == kw and 2 * heads <= HEAD_DIM

    w_main = jnp.concatenate([w_in[:, :, :4 * kw], w_in[:, :, 4 * kw + 2 * heads:]], axis=2).astype(BF16)
    w_ba = jnp.pad(w_in[:, :, 4 * kw:4 * kw + 2 * heads],
                   ((0, 0), (0, 0), (0, HEAD_DIM - 2 * heads))).astype(BF16)
    w_out_b = w_out.astype(BF16)
    w_up_b = w_up.astype(BF16)
    w_down_b = w_down.astype(BF16)

    lead = CHUNK - n_meta
    seq_rows = CHUNK + seq
    xp = jnp.concatenate([jnp.zeros((batch, lead, d), x_prompt.dtype),
                          jnp.broadcast_to(meta_tokens.astype(x_prompt.dtype)[None], (batch, n_meta, d)),
                          x_prompt], axis=1).reshape(batch * seq_rows, d)
    xs = x_sample.reshape(n_seq * toks, d)

    def first_norm(x, name):
        tm = _pick_tile(x.shape[0], 512)
        return _norm(x, norm1_w[0], BF16, tm, (x.shape[0] // tm,), lambda i: (i, 0), x.shape[0],
                     lambda i: (i, 0), name)

    hp = first_norm(xp, "norm_in_prompt")
    hs = first_norm(xs, "norm_in_sample")
    conv_p, gdn_p, hg_p, conv_s = [], [], [], []
    carried = None
    for l in range(depth):
        params = _mixer_params(l, heads, conv_w, a_log, dt_bias, gdn_norm_w, hgrn_norm_w, hgrn_lb)
        next_w = norm1_w[l + 1] if l + 1 < depth else final_norm_w
        proj, ba = _proj(hp, w_main[l], w_ba[l], f"proj_prompt_l{l}")
        mix, c_new, g_new, h_new = _mix_prompt(proj, ba, params, l, batch, heads)
        x1, h2 = _wout(xp, mix, w_out_b[l], norm2_w[l], f"wout_prompt_l{l}")
        xp, hp = _mlp(x1, h2, w_up_b[l], w_down_b[l], next_w, f"mlp_prompt_l{l}")
        conv_p.append(c_new); gdn_p.append(g_new); hg_p.append(h_new)
        proj, ba = _proj(hs, w_main[l], w_ba[l], f"proj_sample_l{l}")
        mix, c_new, g_new, h_new = _mix_sample(proj, ba, state_gdn_conv, state_gdn, state_hgrn, params, l,
                                               n_seq, toks, heads, carried)
        carried = (g_new, h_new)
        x1, h2 = _wout(xs, mix, w_out_b[l], norm2_w[l], f"wout_sample_l{l}")
        xs, hs = _mlp(x1, h2, w_up_b[l], w_down_b[l], next_w, f"mlp_sample_l{l}")
        conv_s.append(c_new)

    blocks = seq // CHUNK
    yp = _norm(xp, final_norm_w, x_prompt.dtype, CHUNK, (batch, blocks),
               lambda b, r: (b * (blocks + 1) + 1 + r, 0), batch * seq,
               lambda b, r: (b * blocks + r, 0), "norm_out_prompt").reshape(batch, seq, d)
    tm = _pick_tile(n_seq * toks, 512)
    ys = _norm(xs, final_norm_w, x_sample.dtype, tm, (n_seq * toks // tm,), lambda i: (i, 0), n_seq * toks,
               lambda i: (i, 0), "norm_out_sample").reshape(n_seq, toks, d)
    return (yp, ys, jnp.stack(conv_p).astype(x_prompt.dtype), jnp.stack(gdn_p).astype(state_gdn.dtype),
            jnp.stack(hg_p).astype(state_hgrn.dtype), jnp.stack(conv_s).astype(state_gdn_conv.dtype),
            carried[0].astype(state_gdn.dtype), carried[1].astype(state_hgrn.dtype))
```

```python
import functools
import math

import numpy as np
import jax
import jax.numpy as jnp
from jax import lax
from jax.experimental import pallas as pl
from jax.experimental.pallas import tpu as pltpu

HEAD_DIM = 128
CHUNK = 64
CONV_W = 4
EPS = 1e-6
SAMPLE_SEQS_PER_STEP = 4
PROMPT_SEQS_PER_STEP = 2
V7X_VMEM_LIMIT_BYTES = 56 * 1024 * 1024

F32 = jnp.float32
BF16 = jnp.bfloat16


def _mm(a, b):
    return jnp.dot(a.astype(BF16), b.astype(BF16), preferred_element_type=F32)


def _mm_nt(a, b):
    return lax.dot_general(a.astype(BF16), b.astype(BF16), (((1,), (1,)), ((), ())),
                           preferred_element_type=F32)


def _mm_tn(a, b):
    return lax.dot_general(a.astype(BF16), b.astype(BF16), (((0,), (0,)), ((), ())),
                           preferred_element_type=F32)


def _split_bf16(x):
    hi = x.astype(BF16)
    return hi, (x - hi.astype(F32)).astype(BF16)


def _mm_3pass(a, b):
    ah, al = _split_bf16(a)
    bh, bl = _split_bf16(b)
    return (jnp.dot(ah, bh, preferred_element_type=F32) + jnp.dot(ah, bl, preferred_element_type=F32)
            + jnp.dot(al, bh, preferred_element_type=F32))


def _mm_sel(m01, x):
    x1 = x.astype(BF16)
    r1 = x - x1.astype(F32)
    x2 = r1.astype(BF16)
    x3 = (r1 - x2.astype(F32)).astype(BF16)
    return (jnp.dot(m01, x1, preferred_element_type=F32)
            + jnp.dot(m01, x2, preferred_element_type=F32)
            + jnp.dot(m01, x3, preferred_element_type=F32))


def _sigmoid(x):
    return 1.0 / (1.0 + jnp.exp(-x))


def _sigmoid_abs(x):
    return 0.5 * jnp.tanh(0.5 * x) + 0.5


def _silu(x):
    return x * _sigmoid_abs(x)


def _softplus(x):
    return jnp.maximum(x, 0.0) + jnp.log(1.0 + jnp.exp(-jnp.abs(x)))


def _pick_tile(m, target, mult=16):
    best = None
    for t in range(mult, m + 1, mult):
        if m % t == 0 and t <= target:
            best = t
    if best is None:
        raise ValueError(f"no tile for {m}")
    return best


def _cparams(sem):
    return pltpu.CompilerParams(dimension_semantics=sem, vmem_limit_bytes=V7X_VMEM_LIMIT_BYTES)


@functools.lru_cache(maxsize=None)
def _block_constants(groups, toks):
    rows = groups * toks
    grp = np.arange(rows) // toks
    pos = np.arange(rows) % toks
    same = grp[:, None] == grp[None, :]
    pt, ps = pos[:, None], pos[None, :]
    prefix = (same & (ps <= pt)).astype(np.float32)
    lev = np.full((rows, rows), -1, np.int32)
    lev[np.arange(rows), np.arange(rows)] = 0
    n, idx = toks, 1
    while n >= 2:
        half = n // 2
        blk = same & ((pt // n) == (ps // n))
        lev[blk & (pt % n >= half) & (ps % n < half)] = idx
        n //= 2
        idx += 1
    return prefix, lev, idx - 1


def _rows_from(b, offset, n):
    rows = b.shape[0]
    if n % 8 == 0:
        return jnp.concatenate(
            [jnp.broadcast_to(b[s + offset:s + offset + 1, :], (n, HEAD_DIM)) for s in range(0, rows, n)], axis=0)
    b3 = b.reshape(rows // 8, 8, HEAD_DIM)
    sub = lax.broadcasted_iota(jnp.int32, b3.shape, 1)
    out = None
    for s in range(0, 8, n):
        cand = jnp.broadcast_to(b3[:, s + offset:s + offset + 1, :], b3.shape)
        out = cand if out is None else jnp.where(sub >= s, cand, out)
    return out.reshape(rows, HEAD_DIM)


def _gdn_blocks(qs, ks, vs, betas, bcols, brows, sufcols, lev, states, toks):
    blocks = range(len(qs))
    rows = qs[0].shape[0]
    groups = range(rows // toks)
    gsl = [slice(g * toks, (g + 1) * toks) for g in groups]
    eye = (lev == 0).astype(F32)
    decay = [jnp.exp(jnp.where(lev >= 0, bcols[p] - brows[p], -jnp.inf)) for p in blocks]
    kb = [ks[p] * betas[p] for p in blocks]
    a = [jnp.where(lev >= 1, _mm_nt(kb[p], ks[p]) * decay[p], 0.0) for p in blocks]
    eb = [jnp.exp(bcols[p]) for p in blocks]
    qe = [qs[p] * eb[p] for p in blocks]
    o_state = [jnp.concatenate([_mm(qe[p][gsl[g]], states[p][g]) for g in groups], axis=0) for p in blocks]
    qk = [_mm_nt(qs[p], ks[p]) * decay[p] for p in blocks]
    inv = [eye - a[p] for p in blocks]
    apow = a
    for _ in range(int(math.log2(toks)) - 1):
        apow = [_mm_3pass(x, x) for x in apow]
        inv = [inv[p] + _mm_3pass(inv[p], apow[p]) for p in blocks]
    sol = [_mm_3pass(inv[p], jnp.concatenate([vs[p] * betas[p], kb[p] * eb[p]], axis=1)) for p in blocks]
    v_new = [sol[p][:, :HEAD_DIM]
             - jnp.concatenate([_mm(sol[p][gsl[g], HEAD_DIM:], states[p][g]) for g in groups], axis=0)
             for p in blocks]
    o = [o_state[p] + _mm(qk[p], v_new[p]) for p in blocks]
    kdec = [ks[p] * jnp.exp(sufcols[p]) for p in blocks]
    new_states = []
    for p in blocks:
        per_group = []
        for g in groups:
            r0 = g * toks
            ebl = jnp.exp(bcols[p][r0:r0 + 1] + sufcols[p][r0:r0 + 1])
            per_group.append(ebl * states[p][g] + _mm_tn(kdec[p][gsl[g]], v_new[p][gsl[g]]))
        new_states.append(per_group)
    return o, new_states


def _hgrn_blocks(qs, f_logits, vs, lbs, prefix, lev, n_levels, states_t, toks):
    blocks = range(len(qs))
    rows = qs[0].shape[0]
    groups = range(rows // toks)
    gsl = [slice(g * toks, (g + 1) * toks) for g in groups]
    fg = [lbs[p] + (1.0 - lbs[p]) * _sigmoid(f_logits[p]) for p in blocks]
    ks = [1.0 - fg[p] for p in blocks]
    b = [_mm_sel(prefix, jnp.log(fg[p])) for p in blocks]
    x_pre = [jnp.exp(b[p]) for p in blocks]
    x_suf = [jnp.exp(-jnp.abs(b[p] - _rows_from(b[p], toks - 1, toks))) for p in blocks]
    a = [jnp.where(lev == 0, _mm_nt(qs[p], ks[p]), 0.0) for p in blocks]
    for i in range(n_levels):
        n = toks >> i
        xl = [jnp.exp(-jnp.abs(b[p] - _rows_from(b[p], n // 2 - 1, n))) for p in blocks]
        a = [a[p] + jnp.where(lev == i + 1, _mm_nt(qs[p] * xl[p], ks[p] * xl[p]), 0.0) for p in blocks]
    o = [jnp.concatenate([_mm_nt((qs[p] * x_pre[p])[gsl[g]], states_t[p][g]) for g in groups], axis=0)
         + _mm(a[p], vs[p]) for p in blocks]
    new_states = []
    for p in blocks:
        kdec = ks[p] * x_suf[p]
        per_group = []
        for g in groups:
            last = g * toks + toks - 1
            per_group.append(states_t[p][g] * x_pre[p][last:last + 1] + _mm_tn(vs[p][gsl[g]], kdec[gsl[g]]))
        new_states.append(per_group)
    return o, new_states


def _head_norm_gate(o, z, w):
    y = o * lax.rsqrt(jnp.mean(o * o, axis=-1, keepdims=True) + EPS)
    return y * w * _silu(z)


def _l2norm(x):
    return x * lax.rsqrt(jnp.sum(x * x, axis=-1, keepdims=True) + EPS)


def _layer_lower_bound(lbp, layer):
    m = jnp.max(lbp, axis=0, keepdims=True)
    e = jnp.exp(lbp - m)
    p = e / jnp.sum(e, axis=0, keepdims=True)
    if layer == 0:
        return jnp.zeros_like(m)
    return jnp.sum(p[1:layer + 1], axis=0, keepdims=True)


def _conv_slab(cbuf, base, rows, cols, w):
    y = (w[3:4] * cbuf[base:base + rows, cols] + w[2:3] * cbuf[base - 1:base - 1 + rows, cols]
         + w[1:2] * cbuf[base - 2:base - 2 + rows, cols] + w[0:1] * cbuf[base - 3:base - 3 + rows, cols])
    return _silu(y)


def _mix_prompt_kernel(pg_ref, ph_ref, ba_ref, convw_ref, alog_ref, dtb_ref, gnw_ref, hnw_ref, lbp_ref,
                       sel_ref, lev_ref, mix_ref, convo_ref, sgo_ref, sho_ref,
                       cbuf, sg, sht, *, heads, layer, n_levels, seqs):
    c = pl.program_id(1)
    kw = heads * HEAD_DIM
    rows = CHUNK

    @pl.when(c == 0)
    def _():
        cbuf[:, 0:8, :] = jnp.zeros((seqs, 8, 3 * kw), F32)
        sg[...] = jnp.zeros_like(sg)
        sht[...] = jnp.zeros_like(sht)

    sel = sel_ref[...]
    lev = lev_ref[...]
    gnw = gnw_ref[...]
    hnw = hnw_ref[...]
    scale = HEAD_DIM ** -0.5
    ps = [(s, h) for s in range(seqs) for h in range(heads)]

    def slab(off, h):
        return slice(off + h * HEAD_DIM, off + (h + 1) * HEAD_DIM)

    def conv(s, off, h):
        return _conv_slab(cbuf.at[s], 8, rows, slab(off, h), convw_ref[:, slab(off, h)])

    for s in range(seqs):
        cbuf[s, 8:8 + rows, :] = pg_ref[s, :, 0:3 * kw]
    qs = [_l2norm(conv(s, 0, h)) * scale for s, h in ps]
    ks = [_l2norm(conv(s, kw, h)) for s, h in ps]
    vs = [conv(s, 2 * kw, h) for s, h in ps]
    for s in range(seqs):
        tail = cbuf[s, 8 + rows - 3:8 + rows, :]
        cbuf[s, 5:8, :] = tail
        convo_ref[s] = tail

    beta_all, b_all, b_all_t, suf_all = [], [], [], []
    for s in range(seqs):
        ba = ba_ref[s]
        beta_all.append(_sigmoid_abs(ba))
        g_all = -jnp.exp(alog_ref[...]) * _softplus(ba + dtb_ref[...])
        b = _mm_sel(sel, g_all)
        b_all.append(b)
        b_all_t.append(b.T)
        suf_all.append(b[rows - 1:rows, :] - b)
    os_, new = _gdn_blocks(qs, ks, vs, [beta_all[s][:, h:h + 1] for s, h in ps],
                           [b_all[s][:, heads + h:heads + h + 1] for s, h in ps],
                           [b_all_t[s][heads + h:heads + h + 1, :] for s, h in ps],
                           [suf_all[s][:, heads + h:heads + h + 1] for s, h in ps], lev,
                           [[sg[s, h]] for s, h in ps], rows)
    for p, (s, h) in enumerate(ps):
        sg[s, h] = new[p][0]
        mix_ref[s, :, slab(0, h)] = _head_norm_gate(os_[p], pg_ref[s, :, slab(3 * kw, h)],
                                                    gnw).astype(mix_ref.dtype)

    lb_all = _layer_lower_bound(lbp_ref[...], layer)
    os_, new = _hgrn_blocks([ph_ref[s, :, slab(0, h)] for s, h in ps],
                            [ph_ref[s, :, slab(kw, h)] for s, h in ps],
                            [ph_ref[s, :, slab(2 * kw, h)] for s, h in ps],
                            [lb_all[:, slab(0, h)] for s, h in ps], sel, lev, n_levels,
                            [[sht[s, h]] for s, h in ps], rows)
    for p, (s, h) in enumerate(ps):
        sht[s, h] = new[p][0]
        mix_ref[s, :, slab(kw, h)] = _head_norm_gate(os_[p], ph_ref[s, :, slab(3 * kw, h)],
                                                     hnw).astype(mix_ref.dtype)

    @pl.when(c == pl.num_programs(1) - 1)
    def _():
        sgo_ref[...] = sg[...]
        for s, h in ps:
            sho_ref[s, h] = sht[s, h].T


def _mix_sample_kernel(pg_ref, ph_ref, ba_ref, convs_ref, sgi_ref, shi_ref, convw_ref, alog_ref, dtb_ref,
                       gnw_ref, hnw_ref, lbp_ref, sel_ref, lev_ref, *rest,
                       heads, layer, n_levels, toks, seqs):
    mix_ref, convo_ref, sgo_ref, sho_ref, cbuf = rest[-5:]
    kw = heads * HEAD_DIM
    sel = sel_ref[...]
    lev = lev_ref[...]
    rows = heads * toks
    gnw = gnw_ref[...]
    hnw = hnw_ref[...]
    scale = HEAD_DIM ** -0.5
    lb_all = _layer_lower_bound(lbp_ref[...], layer)
    lb_rows = jnp.concatenate(
        [jnp.broadcast_to(lb_all[:, h * HEAD_DIM:(h + 1) * HEAD_DIM], (toks, HEAD_DIM)) for h in range(heads)],
        axis=0)

    ss = range(seqs)
    hs = range(heads)
    trs = [slice(s * toks, (s + 1) * toks) for s in ss]

    def slab(off, h):
        return slice(off + h * HEAD_DIM, off + (h + 1) * HEAD_DIM)

    def stack(ref, s, off):
        return jnp.concatenate([ref[trs[s], slab(off, h)] for h in hs], axis=0)

    def conv_stack(s, off):
        return jnp.concatenate(
            [_conv_slab(cbuf.at[s], 8, toks, slab(off, h), convw_ref[:, slab(off, h)]) for h in hs], axis=0)

    for s in ss:
        cbuf[s, 5:8, :] = convs_ref[0, s]
        cbuf[s, 8:8 + toks, :] = pg_ref[trs[s], 0:3 * kw]
    qs = [_l2norm(conv_stack(s, 0)) * scale for s in ss]
    ks = [_l2norm(conv_stack(s, kw)) for s in ss]
    vs = [conv_stack(s, 2 * kw) for s in ss]
    for s in ss:
        convo_ref[s] = cbuf[s, 8 + toks - 3:8 + toks, :]

    betas, bcols, brows, sufcols = [], [], [], []
    for s in ss:
        ba = ba_ref[trs[s], :]
        beta_all = _sigmoid_abs(ba)
        g_all = -jnp.exp(alog_ref[...]) * _softplus(ba + dtb_ref[...])
        betas.append(jnp.concatenate([beta_all[:, h:h + 1] for h in hs], axis=0))
        gcol = jnp.concatenate([g_all[:, heads + h:heads + h + 1] for h in hs], axis=0)
        b_b = _mm_sel(sel, jnp.broadcast_to(gcol, (rows, HEAD_DIM)))
        bcols.append(b_b[:, 0:1])
        brows.append(b_b.T[0:1, :])
        sufcols.append((_rows_from(b_b, toks - 1, toks) - b_b)[:, 0:1])
    os_, new = _gdn_blocks(qs, ks, vs, betas, bcols, brows, sufcols, lev,
                           [[sgi_ref[0, s, h] for h in hs] for s in ss], toks)
    for s in ss:
        og = _head_norm_gate(os_[s], stack(pg_ref, s, 3 * kw), gnw)
        for h in hs:
            sgo_ref[0, s, h] = new[s][h]
            mix_ref[trs[s], slab(0, h)] = og[h * toks:(h + 1) * toks].astype(mix_ref.dtype)

    os_, new = _hgrn_blocks([stack(ph_ref, s, 0) for s in ss], [stack(ph_ref, s, kw) for s in ss],
                            [stack(ph_ref, s, 2 * kw) for s in ss], [lb_rows for s in ss], sel, lev, n_levels,
                            [[shi_ref[0, s, h].T for h in hs] for s in ss], toks)
    for s in ss:
        oh = _head_norm_gate(os_[s], stack(ph_ref, s, 3 * kw), hnw)
        for h in hs:
            sho_ref[0, s, h] = new[s][h].T
            mix_ref[trs[s], slab(kw, h)] = oh[h * toks:(h + 1) * toks].astype(mix_ref.dtype)


def _mixer_params(layer, heads, conv_w, a_log, dt_bias, gdn_norm_w, hgrn_norm_w, hgrn_lb):
    pad = HEAD_DIM - 2 * heads
    alog = jnp.pad(a_log[layer][None, :], ((0, 0), (heads, pad)))
    dtb = jnp.pad(dt_bias[layer][None, :], ((0, 0), (heads, pad)))
    return (conv_w[layer], alog, dtb, gdn_norm_w[layer][None, :], hgrn_norm_w[layer][None, :], hgrn_lb)


def _full_spec(a):
    nd = a.ndim
    return pl.BlockSpec(a.shape, lambda *_: (0,) * nd)


def _mix_prompt(projs, params, layer, batch, heads):
    n_rows = projs[0].shape[0]
    seq_rows = n_rows // batch
    seqs = PROMPT_SEQS_PER_STEP if batch % PROMPT_SEQS_PER_STEP == 0 else 1
    kw = heads * HEAD_DIM
    sel_np, lev_np, n_levels = _block_constants(1, CHUNK)
    sel = jnp.asarray(sel_np, BF16)
    lev = jnp.asarray(lev_np)
    row_map = lambda b, c: (b, c, 0)
    consts = params + (sel, lev)
    projs = [p.reshape(batch, seq_rows, p.shape[1]) for p in projs]
    kern = functools.partial(_mix_prompt_kernel, heads=heads, layer=layer, n_levels=n_levels, seqs=seqs)
    mix, conv_new, gdn_new, hg_new = pl.pallas_call(
        kern,
        grid=(batch // seqs, seq_rows // CHUNK),
        in_specs=[pl.BlockSpec((seqs, CHUNK, p.shape[2]), row_map) for p in projs]
                 + [_full_spec(a) for a in consts],
        out_specs=[pl.BlockSpec((seqs, CHUNK, 2 * kw), row_map),
                   pl.BlockSpec((seqs, CONV_W - 1, 3 * kw), lambda b, c: (b, 0, 0)),
                   pl.BlockSpec((seqs, heads, HEAD_DIM, HEAD_DIM), lambda b, c: (b, 0, 0, 0)),
                   pl.BlockSpec((seqs, heads, HEAD_DIM, HEAD_DIM), lambda b, c: (b, 0, 0, 0))],
        out_shape=[jax.ShapeDtypeStruct((batch, seq_rows, 2 * kw), BF16),
                   jax.ShapeDtypeStruct((batch, CONV_W - 1, 3 * kw), F32),
                   jax.ShapeDtypeStruct((batch, heads, HEAD_DIM, HEAD_DIM), F32),
                   jax.ShapeDtypeStruct((batch, heads, HEAD_DIM, HEAD_DIM), F32)],
        scratch_shapes=[pltpu.VMEM((seqs, 8 + CHUNK, 3 * kw), F32),
                        pltpu.VMEM((seqs, heads, HEAD_DIM, HEAD_DIM), F32),
                        pltpu.VMEM((seqs, heads, HEAD_DIM, HEAD_DIM), F32)],
        compiler_params=_cparams(("arbitrary", "arbitrary")),
        name=f"mix_prompt_l{layer}",
    )(*projs, *consts)
    return mix.reshape(n_rows, 2 * kw), conv_new, gdn_new, hg_new


def _mix_sample(projs, conv_state, gdn_state, hgrn_state, params, layer, n_seq, toks, heads, carried):
    kw = heads * HEAD_DIM
    seqs = SAMPLE_SEQS_PER_STEP
    sel_np, lev_np, n_levels = _block_constants(heads, toks)
    sel = jnp.asarray(sel_np, BF16)
    lev = jnp.asarray(lev_np)
    consts = params + (sel, lev)
    row_map = lambda n: (n, 0)
    kern = functools.partial(_mix_sample_kernel, heads=heads, layer=layer, n_levels=n_levels, toks=toks,
                             seqs=seqs)
    state_blk = pl.BlockSpec((1, seqs, heads, HEAD_DIM, HEAD_DIM), lambda n: (layer, n, 0, 0, 0))
    in_specs = [pl.BlockSpec((seqs * toks, p.shape[1]), row_map) for p in projs] + [
        pl.BlockSpec((1, seqs, CONV_W - 1, 3 * kw), lambda n: (layer, n, 0, 0)),
        state_blk, state_blk] + [_full_spec(a) for a in consts]
    args = tuple(projs) + (conv_state, gdn_state, hgrn_state) + consts
    aliases = {}
    if carried is not None:
        aliases = {len(args): 2, len(args) + 1: 3}
        in_specs += [pl.BlockSpec(memory_space=pl.ANY), pl.BlockSpec(memory_space=pl.ANY)]
        args += tuple(carried)
    return pl.pallas_call(
        kern,
        grid=(n_seq // seqs,),
        in_specs=in_specs,
        out_specs=[pl.BlockSpec((seqs * toks, 2 * kw), row_map),
                   pl.BlockSpec((seqs, CONV_W - 1, 3 * kw), lambda n: (n, 0, 0)),
                   state_blk, state_blk],
        out_shape=[jax.ShapeDtypeStruct((n_seq * toks, 2 * kw), BF16),
                   jax.ShapeDtypeStruct((n_seq, CONV_W - 1, 3 * kw), F32),
                   jax.ShapeDtypeStruct(gdn_state.shape, F32),
                   jax.ShapeDtypeStruct(hgrn_state.shape, F32)],
        scratch_shapes=[pltpu.VMEM((seqs, 8 + toks, 3 * kw), F32)],
        input_output_aliases=aliases,
        compiler_params=_cparams(("arbitrary",)),
        name=f"mix_sample_l{layer}",
    )(*args)


def _rmsnorm_rows(x, w):
    return x * lax.rsqrt(jnp.mean(x * x, axis=-1, keepdims=True) + EPS) * w


def _norm_kernel(x_ref, w_ref, o_ref):
    o_ref[...] = _rmsnorm_rows(x_ref[...], w_ref[...]).astype(o_ref.dtype)


def _norm(x, w, out_dtype, block_rows, grid, in_map, out_rows, out_map, name):
    d = x.shape[1]
    return pl.pallas_call(
        _norm_kernel,
        grid=grid,
        in_specs=[pl.BlockSpec((block_rows, d), in_map), pl.BlockSpec((1, d), lambda *_: (0, 0))],
        out_specs=pl.BlockSpec((block_rows, d), out_map),
        out_shape=jax.ShapeDtypeStruct((out_rows, d), out_dtype),
        compiler_params=_cparams(("arbitrary",) * len(grid)),
        name=name,
    )(x, w[None, :])


def _proj_kernel(h_ref, w_ref, o_ref):
    o_ref[...] = jnp.dot(h_ref[...], w_ref[...], preferred_element_type=F32)


def _proj(h, w, layer, name):
    m, d = h.shape
    n = w.shape[2]
    tm = _pick_tile(m, 1056)
    tn = _pick_tile(n, 1024, HEAD_DIM)
    return pl.pallas_call(
        _proj_kernel,
        grid=(m // tm, n // tn),
        in_specs=[pl.BlockSpec((tm, d), lambda i, j: (i, 0)),
                  pl.BlockSpec((None, d, tn), lambda i, j: (layer, 0, j))],
        out_specs=pl.BlockSpec((tm, tn), lambda i, j: (i, j)),
        out_shape=jax.ShapeDtypeStruct((m, n), F32),
        compiler_params=_cparams(("arbitrary", "arbitrary")),
        name=name,
    )(h, w)


def _wout_kernel(x_ref, mix_ref, w_ref, nw_ref, x1_ref, h_ref):
    x1 = x_ref[...] + jnp.dot(mix_ref[...], w_ref[...], preferred_element_type=F32)
    x1_ref[...] = x1
    h_ref[...] = _rmsnorm_rows(x1, nw_ref[...]).astype(h_ref.dtype)


def _wout(x, mix, w_out, layer, norm_w, name):
    m, d = x.shape
    tm = _pick_tile(m, 352)
    return pl.pallas_call(
        _wout_kernel,
        grid=(m // tm,),
        in_specs=[pl.BlockSpec((tm, d), lambda i: (i, 0)),
                  pl.BlockSpec((tm, mix.shape[1]), lambda i: (i, 0)),
                  pl.BlockSpec((None,) + w_out.shape[1:], lambda i: (layer, 0, 0)),
                  pl.BlockSpec((1, d), lambda i: (0, 0))],
        out_specs=[pl.BlockSpec((tm, d), lambda i: (i, 0)), pl.BlockSpec((tm, d), lambda i: (i, 0))],
        out_shape=[jax.ShapeDtypeStruct((m, d), F32), jax.ShapeDtypeStruct((m, d), BF16)],
        compiler_params=_cparams(("arbitrary",)),
        name=name,
    )(x, mix, w_out, norm_w[None, :])


def _mlp_kernel(x_ref, h_ref, wup_ref, wdn_ref, nw_ref, xo_ref, ho_ref):
    f = pl.program_id(1)

    @pl.when(f == 0)
    def _():
        xo_ref[...] = x_ref[...]

    u = jnp.maximum(jnp.dot(h_ref[...], wup_ref[...], preferred_element_type=F32), 0.0)
    xo_ref[...] += jnp.dot((u * u).astype(BF16), wdn_ref[...], preferred_element_type=F32)

    @pl.when(f == pl.num_programs(1) - 1)
    def _():
        ho_ref[...] = _rmsnorm_rows(xo_ref[...], nw_ref[...]).astype(ho_ref.dtype)


def _mlp(x, h, w_up, w_down, layer, next_norm_w, name):
    m, d = x.shape
    ff = w_up.shape[2]
    tm = _pick_tile(m, 704)
    tf = _pick_tile(ff, 512, HEAD_DIM)
    return pl.pallas_call(
        _mlp_kernel,
        grid=(m // tm, ff // tf),
        in_specs=[pl.BlockSpec((tm, d), lambda i, f: (i, 0)),
                  pl.BlockSpec((tm, d), lambda i, f: (i, 0)),
                  pl.BlockSpec((None, d, tf), lambda i, f: (layer, 0, f)),
                  pl.BlockSpec((None, tf, d), lambda i, f: (layer, f, 0)),
                  pl.BlockSpec((1, d), lambda i, f: (0, 0))],
        out_specs=[pl.BlockSpec((tm, d), lambda i, f: (i, 0)), pl.BlockSpec((tm, d), lambda i, f: (i, 0))],
        out_shape=[jax.ShapeDtypeStruct((m, d), F32), jax.ShapeDtypeStruct((m, d), BF16)],
        compiler_params=_cparams(("arbitrary", "arbitrary")),
        name=name,
    )(x, h, w_up, w_down, next_norm_w[None, :])


def kernel(x_prompt, x_sample, state_gdn_conv, state_gdn, state_hgrn, meta_tokens, norm1_w, w_in, conv_w,
           a_log, dt_bias, gdn_norm_w, hgrn_lb, hgrn_norm_w, w_out, norm2_w, w_up, w_down, final_norm_w):
    batch, seq, d = x_prompt.shape
    n_seq, toks, _ = x_sample.shape
    depth = w_in.shape[0]
    heads = a_log.shape[1]
    kw = heads * HEAD_DIM
    n_meta = meta_tokens.shape[0]
    assert seq % CHUNK == 0 and 0 < n_meta <= CHUNK and n_seq % SAMPLE_SEQS_PER_STEP == 0
    assert w_in.shape[2] == 8 * kw + 2 * heads and hgrn_lb.shape[1] == kw and 2 * heads <= HEAD_DIM

    w_gdn = w_in[:, :, :4 * kw].astype(BF16)
    w_hg = w_in[:, :, 4 * kw + 2 * heads:].astype(BF16)
    w_ba = jnp.pad(w_in[:, :, 4 * kw:4 * kw + 2 * heads],
                   ((0, 0), (0, 0), (0, HEAD_DIM - 2 * heads))).astype(BF16)
    w_out_b = w_out.astype(BF16)
    w_up_b = w_up.astype(BF16)
    w_down_b = w_down.astype(BF16)

    lead = CHUNK - n_meta
    seq_rows = CHUNK + seq
    xp = jnp.concatenate([jnp.zeros((batch, lead, d), x_prompt.dtype),
                          jnp.broadcast_to(meta_tokens.astype(x_prompt.dtype)[None], (batch, n_meta, d)),
                          x_prompt], axis=1).reshape(batch * seq_rows, d)
    xs = x_sample.reshape(n_seq * toks, d)

    def first_norm(x, name):
        tm = _pick_tile(x.shape[0], 512)
        return _norm(x, norm1_w[0], BF16, tm, (x.shape[0] // tm,), lambda i: (i, 0), x.shape[0],
                     lambda i: (i, 0), name)

    hp = first_norm(xp, "norm_in_prompt")
    hs = first_norm(xs, "norm_in_sample")
    conv_p, gdn_p, hg_p, conv_s = [], [], [], []
    carried = None
    for l in range(depth):
        params = _mixer_params(l, heads, conv_w, a_log, dt_bias, gdn_norm_w, hgrn_norm_w, hgrn_lb)
        next_w = norm1_w[l + 1] if l + 1 < depth else final_norm_w
        projs = (_proj(hp, w_gdn, l, f"proj_gdn_prompt_l{l}"), _proj(hp, w_hg, l, f"proj_hgrn_prompt_l{l}"),
                 _proj(hp, w_ba, l, f"proj_ba_prompt_l{l}"))
        mix, c_new, g_new, h_new = _mix_prompt(projs, params, l, batch, heads)
        x1, h2 = _wout(xp, mix, w_out_b, l, norm2_w[l], f"wout_prompt_l{l}")
        xp, hp = _mlp(x1, h2, w_up_b, w_down_b, l, next_w, f"mlp_prompt_l{l}")
        conv_p.append(c_new); gdn_p.append(g_new); hg_p.append(h_new)
        projs = (_proj(hs, w_gdn, l, f"proj_gdn_sample_l{l}"), _proj(hs, w_hg, l, f"proj_hgrn_sample_l{l}"),
                 _proj(hs, w_ba, l, f"proj_ba_sample_l{l}"))
        mix, c_new, g_new, h_new = _mix_sample(projs, state_gdn_conv, state_gdn, state_hgrn, params, l,
                                               n_seq, toks, heads, carried)
        carried = (g_new, h_new)
        x1, h2 = _wout(xs, mix, w_out_b, l, norm2_w[l], f"wout_sample_l{l}")
        xs, hs = _mlp(x1, h2, w_up_b, w_down_b, l, next_w, f"mlp_sample_l{l}")
        conv_s.append(c_new)

    blocks = seq // CHUNK
    yp = _norm(xp, final_norm_w, x_prompt.dtype, CHUNK, (batch, blocks),
               lambda b, r: (b * (blocks + 1) + 1 + r, 0), batch * seq,
               lambda b, r: (b * blocks + r, 0), "norm_out_prompt").reshape(batch, seq, d)
    tm = _pick_tile(n_seq * toks, 512)
    ys = _norm(xs, final_norm_w, x_sample.dtype, tm, (n_seq * toks // tm,), lambda i: (i, 0), n_seq * toks,
               lambda i: (i, 0), "norm_out_sample").reshape(n_seq, toks, d)
    return (yp, ys, jnp.stack(conv_p).astype(x_prompt.dtype), jnp.stack(gdn_p).astype(state_gdn.dtype),
            jnp.stack(hg_p).astype(state_hgrn.dtype), jnp.stack(conv_s).astype(state_gdn_conv.dtype),
            carried[0].astype(state_gdn.dtype), carried[1].astype(state_hgrn.dtype))
```

```python
import functools
import math

import numpy as np
import jax
import jax.numpy as jnp
from jax import lax
from jax.experimental import pallas as pl
from jax.experimental.pallas import tpu as pltpu

HEAD_DIM = 128
CHUNK = 64
CONV_W = 4
EPS = 1e-6
SAMPLE_SEQS_PER_STEP = 4
PROMPT_SEQS_PER_STEP = 2
V7X_VMEM_LIMIT_BYTES = 56 * 1024 * 1024

F32 = jnp.float32
BF16 = jnp.bfloat16


def _mm(a, b):
    return jnp.dot(a.astype(BF16), b.astype(BF16), preferred_element_type=F32)


def _mm_nt(a, b):
    return lax.dot_general(a.astype(BF16), b.astype(BF16), (((1,), (1,)), ((), ())),
                           preferred_element_type=F32)


def _mm_tn(a, b):
    return lax.dot_general(a.astype(BF16), b.astype(BF16), (((0,), (0,)), ((), ())),
                           preferred_element_type=F32)


def _split_bf16(x):
    hi = x.astype(BF16)
    return hi, (x - hi.astype(F32)).astype(BF16)


def _mm_3pass(a, b):
    ah, al = _split_bf16(a)
    bh, bl = _split_bf16(b)
    return (jnp.dot(ah, bh, preferred_element_type=F32) + jnp.dot(ah, bl, preferred_element_type=F32)
            + jnp.dot(al, bh, preferred_element_type=F32))


def _mm_sel(m01, x):
    x1 = x.astype(BF16)
    r1 = x - x1.astype(F32)
    x2 = r1.astype(BF16)
    x3 = (r1 - x2.astype(F32)).astype(BF16)
    return (jnp.dot(m01, x1, preferred_element_type=F32)
            + jnp.dot(m01, x2, preferred_element_type=F32)
            + jnp.dot(m01, x3, preferred_element_type=F32))


def _sigmoid(x):
    return 1.0 / (1.0 + jnp.exp(-x))


def _sigmoid_abs(x):
    return 0.5 * jnp.tanh(0.5 * x) + 0.5


def _silu(x):
    h = 0.5 * x
    return h + h * jnp.tanh(h)


def _softplus(x):
    return jnp.maximum(x, 0.0) + jnp.log(1.0 + jnp.exp(-jnp.abs(x)))


def _pick_tile(m, target, mult=16):
    best = None
    for t in range(mult, m + 1, mult):
        if m % t == 0 and t <= target:
            best = t
    if best is None:
        raise ValueError(f"no tile for {m}")
    return best


def _cparams(sem):
    return pltpu.CompilerParams(dimension_semantics=sem, vmem_limit_bytes=V7X_VMEM_LIMIT_BYTES)


@functools.lru_cache(maxsize=None)
def _block_constants(groups, toks):
    rows = groups * toks
    grp = np.arange(rows) // toks
    pos = np.arange(rows) % toks
    same = grp[:, None] == grp[None, :]
    pt, ps = pos[:, None], pos[None, :]
    prefix = (same & (ps <= pt)).astype(np.float32)
    lev = np.full((rows, rows), -1, np.int32)
    lev[np.arange(rows), np.arange(rows)] = 0
    n, idx = toks, 1
    while n >= 2:
        half = n // 2
        blk = same & ((pt // n) == (ps // n))
        lev[blk & (pt % n >= half) & (ps % n < half)] = idx
        n //= 2
        idx += 1
    return prefix, lev, idx - 1


def _rows_from(b, offset, n):
    rows = b.shape[0]
    if n % 8 == 0:
        return jnp.concatenate(
            [jnp.broadcast_to(b[s + offset:s + offset + 1, :], (n, HEAD_DIM)) for s in range(0, rows, n)], axis=0)
    b3 = b.reshape(rows // 8, 8, HEAD_DIM)
    sub = lax.broadcasted_iota(jnp.int32, b3.shape, 1)
    out = None
    for s in range(0, 8, n):
        cand = jnp.broadcast_to(b3[:, s + offset:s + offset + 1, :], b3.shape)
        out = cand if out is None else jnp.where(sub >= s, cand, out)
    return out.reshape(rows, HEAD_DIM)


def _gdn_blocks(qs, ks, vs, betas, bcols, brows, sufcols, lev, states, toks):
    blocks = range(len(qs))
    rows = qs[0].shape[0]
    groups = range(rows // toks)
    gsl = [slice(g * toks, (g + 1) * toks) for g in groups]
    eye = (lev == 0).astype(F32)
    decay = [jnp.exp(jnp.where(lev >= 0, bcols[p] - brows[p], -jnp.inf)) for p in blocks]
    kb = [ks[p] * betas[p] for p in blocks]
    a = [jnp.where(lev >= 1, _mm_nt(kb[p], ks[p]) * decay[p], 0.0) for p in blocks]
    eb = [jnp.exp(bcols[p]) for p in blocks]
    qe = [qs[p] * eb[p] for p in blocks]
    o_state = [jnp.concatenate([_mm(qe[p][gsl[g]], states[p][g]) for g in groups], axis=0) for p in blocks]
    qk = [_mm_nt(qs[p], ks[p]) * decay[p] for p in blocks]
    inv = [eye - a[p] for p in blocks]
    apow = a
    for _ in range(int(math.log2(toks)) - 1):
        apow = [_mm_3pass(x, x) for x in apow]
        inv = [inv[p] + _mm_3pass(inv[p], apow[p]) for p in blocks]
    sol = [_mm_3pass(inv[p], jnp.concatenate([vs[p] * betas[p], kb[p] * eb[p]], axis=1)) for p in blocks]
    v_new = [sol[p][:, :HEAD_DIM]
             - jnp.concatenate([_mm(sol[p][gsl[g], HEAD_DIM:], states[p][g]) for g in groups], axis=0)
             for p in blocks]
    o = [o_state[p] + _mm(qk[p], v_new[p]) for p in blocks]
    kdec = [ks[p] * jnp.exp(sufcols[p]) for p in blocks]
    new_states = []
    for p in blocks:
        per_group = []
        for g in groups:
            r0 = g * toks
            ebl = jnp.exp(bcols[p][r0:r0 + 1] + sufcols[p][r0:r0 + 1])
            per_group.append(ebl * states[p][g] + _mm_tn(kdec[p][gsl[g]], v_new[p][gsl[g]]))
        new_states.append(per_group)
    return o, new_states


def _hgrn_blocks(qs, f_logits, vs, lbs, prefix, lev, n_levels, states_t, toks):
    blocks = range(len(qs))
    rows = qs[0].shape[0]
    groups = range(rows // toks)
    gsl = [slice(g * toks, (g + 1) * toks) for g in groups]
    fg = [lbs[p] + (1.0 - lbs[p]) * _sigmoid(f_logits[p]) for p in blocks]
    ks = [1.0 - fg[p] for p in blocks]
    b = [_mm_sel(prefix, jnp.log(fg[p])) for p in blocks]
    x_pre = [jnp.exp(b[p]) for p in blocks]
    x_suf = [jnp.exp(-jnp.abs(b[p] - _rows_from(b[p], toks - 1, toks))) for p in blocks]
    a = [jnp.where(lev == 0, _mm_nt(qs[p], ks[p]), 0.0) for p in blocks]
    for i in range(n_levels):
        n = toks >> i
        xl = [jnp.exp(-jnp.abs(b[p] - _rows_from(b[p], n // 2 - 1, n))) for p in blocks]
        a = [a[p] + jnp.where(lev == i + 1, _mm_nt(qs[p] * xl[p], ks[p] * xl[p]), 0.0) for p in blocks]
    o = [jnp.concatenate([_mm_nt((qs[p] * x_pre[p])[gsl[g]], states_t[p][g]) for g in groups], axis=0)
         + _mm(a[p], vs[p]) for p in blocks]
    new_states = []
    for p in blocks:
        kdec = ks[p] * x_suf[p]
        per_group = []
        for g in groups:
            last = g * toks + toks - 1
            per_group.append(states_t[p][g] * x_pre[p][last:last + 1] + _mm_tn(vs[p][gsl[g]], kdec[gsl[g]]))
        new_states.append(per_group)
    return o, new_states


def _head_norm_gate(o, z, w):
    y = o * lax.rsqrt(jnp.mean(o * o, axis=-1, keepdims=True) + EPS)
    return y * w * _silu(z)


def _l2norm(x):
    return x * lax.rsqrt(jnp.sum(x * x, axis=-1, keepdims=True) + EPS)


def _layer_lower_bound(lbp, layer):
    m = jnp.max(lbp, axis=0, keepdims=True)
    e = jnp.exp(lbp - m)
    p = e / jnp.sum(e, axis=0, keepdims=True)
    if layer == 0:
        return jnp.zeros_like(m)
    return jnp.sum(p[1:layer + 1], axis=0, keepdims=True)


def _conv_slab(cbuf, base, rows, cols, w):
    y = (w[3:4] * cbuf[base:base + rows, cols] + w[2:3] * cbuf[base - 1:base - 1 + rows, cols]
         + w[1:2] * cbuf[base - 2:base - 2 + rows, cols] + w[0:1] * cbuf[base - 3:base - 3 + rows, cols])
    return _silu(y)


def _mix_prompt_kernel(pg_ref, ph_ref, ba_ref, convw_ref, alog_ref, dtb_ref, gnw_ref, hnw_ref, lbp_ref,
                       sel_ref, lev_ref, mix_ref, convo_ref, sgo_ref, sho_ref,
                       cbuf, sg, sht, *, heads, layer, n_levels, seqs):
    c = pl.program_id(1)
    kw = heads * HEAD_DIM
    rows = CHUNK

    @pl.when(c == 0)
    def _():
        cbuf[:, 0:8, :] = jnp.zeros((seqs, 8, 3 * kw), F32)
        sg[...] = jnp.zeros_like(sg)
        sht[...] = jnp.zeros_like(sht)

    sel = sel_ref[...]
    lev = lev_ref[...]
    gnw = gnw_ref[...]
    hnw = hnw_ref[...]
    scale = HEAD_DIM ** -0.5
    ps = [(s, h) for s in range(seqs) for h in range(heads)]

    def slab(off, h):
        return slice(off + h * HEAD_DIM, off + (h + 1) * HEAD_DIM)

    def conv(s, off, h):
        return _conv_slab(cbuf.at[s], 8, rows, slab(off, h), convw_ref[:, slab(off, h)])

    for s in range(seqs):
        cbuf[s, 8:8 + rows, :] = pg_ref[s, :, 0:3 * kw]
    qs = [_l2norm(conv(s, 0, h)) * scale for s, h in ps]
    ks = [_l2norm(conv(s, kw, h)) for s, h in ps]
    vs = [conv(s, 2 * kw, h) for s, h in ps]
    for s in range(seqs):
        tail = cbuf[s, 8 + rows - 3:8 + rows, :]
        cbuf[s, 5:8, :] = tail
        convo_ref[s] = tail

    beta_all, b_all, b_all_t, suf_all = [], [], [], []
    for s in range(seqs):
        ba = ba_ref[s]
        beta_all.append(_sigmoid_abs(ba))
        g_all = -jnp.exp(alog_ref[...]) * _softplus(ba + dtb_ref[...])
        b = _mm_sel(sel, g_all)
        b_all.append(b)
        b_all_t.append(b.T)
        suf_all.append(b[rows - 1:rows, :] - b)
    os_, new = _gdn_blocks(qs, ks, vs, [beta_all[s][:, h:h + 1] for s, h in ps],
                           [b_all[s][:, heads + h:heads + h + 1] for s, h in ps],
                           [b_all_t[s][heads + h:heads + h + 1, :] for s, h in ps],
                           [suf_all[s][:, heads + h:heads + h + 1] for s, h in ps], lev,
                           [[sg[s, h]] for s, h in ps], rows)
    for p, (s, h) in enumerate(ps):
        sg[s, h] = new[p][0]
        mix_ref[s, :, slab(0, h)] = _head_norm_gate(os_[p], pg_ref[s, :, slab(3 * kw, h)],
                                                    gnw).astype(mix_ref.dtype)

    lb_all = _layer_lower_bound(lbp_ref[...], layer)
    os_, new = _hgrn_blocks([ph_ref[s, :, slab(0, h)] for s, h in ps],
                            [ph_ref[s, :, slab(kw, h)] for s, h in ps],
                            [ph_ref[s, :, slab(2 * kw, h)] for s, h in ps],
                            [lb_all[:, slab(0, h)] for s, h in ps], sel, lev, n_levels,
                            [[sht[s, h]] for s, h in ps], rows)
    for p, (s, h) in enumerate(ps):
        sht[s, h] = new[p][0]
        mix_ref[s, :, slab(kw, h)] = _head_norm_gate(os_[p], ph_ref[s, :, slab(3 * kw, h)],
                                                     hnw).astype(mix_ref.dtype)

    @pl.when(c == pl.num_programs(1) - 1)
    def _():
        sgo_ref[...] = sg[...]
        for s, h in ps:
            sho_ref[s, h] = sht[s, h].T


def _mix_sample_kernel(pg_ref, ph_ref, ba_ref, convs_ref, sgi_ref, shi_ref, convw_ref, alog_ref, dtb_ref,
                       gnw_ref, hnw_ref, lbp_ref, sel_ref, lev_ref, *rest,
                       heads, layer, n_levels, toks, seqs):
    mix_ref, convo_ref, sgo_ref, sho_ref, cbuf = rest[-5:]
    kw = heads * HEAD_DIM
    sel = sel_ref[...]
    lev = lev_ref[...]
    rows = heads * toks
    gnw = gnw_ref[...]
    hnw = hnw_ref[...]
    scale = HEAD_DIM ** -0.5
    lb_all = _layer_lower_bound(lbp_ref[...], layer)
    lb_rows = jnp.concatenate(
        [jnp.broadcast_to(lb_all[:, h * HEAD_DIM:(h + 1) * HEAD_DIM], (toks, HEAD_DIM)) for h in range(heads)],
        axis=0)

    ss = range(seqs)
    hs = range(heads)
    trs = [slice(s * toks, (s + 1) * toks) for s in ss]

    def slab(off, h):
        return slice(off + h * HEAD_DIM, off + (h + 1) * HEAD_DIM)

    def stack(ref, s, off):
        return jnp.concatenate([ref[trs[s], slab(off, h)] for h in hs], axis=0)

    def conv_stack(s, off):
        return jnp.concatenate(
            [_conv_slab(cbuf.at[s], 8, toks, slab(off, h), convw_ref[:, slab(off, h)]) for h in hs], axis=0)

    for s in ss:
        cbuf[s, 5:8, :] = convs_ref[0, s]
        cbuf[s, 8:8 + toks, :] = pg_ref[trs[s], 0:3 * kw]
    qs = [_l2norm(conv_stack(s, 0)) * scale for s in ss]
    ks = [_l2norm(conv_stack(s, kw)) for s in ss]
    vs = [conv_stack(s, 2 * kw) for s in ss]
    for s in ss:
        convo_ref[s] = cbuf[s, 8 + toks - 3:8 + toks, :]

    betas, bcols, brows, sufcols = [], [], [], []
    for s in ss:
        ba = ba_ref[trs[s], :]
        beta_all = _sigmoid_abs(ba)
        g_all = -jnp.exp(alog_ref[...]) * _softplus(ba + dtb_ref[...])
        betas.append(jnp.concatenate([beta_all[:, h:h + 1] for h in hs], axis=0))
        gcol = jnp.concatenate([g_all[:, heads + h:heads + h + 1] for h in hs], axis=0)
        b_b = _mm_sel(sel, jnp.broadcast_to(gcol, (rows, HEAD_DIM)))
        bcols.append(b_b[:, 0:1])
        brows.append(b_b.T[0:1, :])
        sufcols.append((_rows_from(b_b, toks - 1, toks) - b_b)[:, 0:1])
    os_, new = _gdn_blocks(qs, ks, vs, betas, bcols, brows, sufcols, lev,
                           [[sgi_ref[0, s, h] for h in hs] for s in ss], toks)
    for s in ss:
        og = _head_norm_gate(os_[s], stack(pg_ref, s, 3 * kw), gnw)
        for h in hs:
            sgo_ref[0, s, h] = new[s][h]
            mix_ref[trs[s], slab(0, h)] = og[h * toks:(h + 1) * toks].astype(mix_ref.dtype)

    os_, new = _hgrn_blocks([stack(ph_ref, s, 0) for s in ss], [stack(ph_ref, s, kw) for s in ss],
                            [stack(ph_ref, s, 2 * kw) for s in ss], [lb_rows for s in ss], sel, lev, n_levels,
                            [[shi_ref[0, s, h].T for h in hs] for s in ss], toks)
    for s in ss:
        oh = _head_norm_gate(os_[s], stack(ph_ref, s, 3 * kw), hnw)
        for h in hs:
            sho_ref[0, s, h] = new[s][h].T
            mix_ref[trs[s], slab(kw, h)] = oh[h * toks:(h + 1) * toks].astype(mix_ref.dtype)


def _mixer_params(layer, heads, conv_w, a_log, dt_bias, gdn_norm_w, hgrn_norm_w, hgrn_lb):
    pad = HEAD_DIM - 2 * heads
    alog = jnp.pad(a_log[layer][None, :], ((0, 0), (heads, pad)))
    dtb = jnp.pad(dt_bias[layer][None, :], ((0, 0), (heads, pad)))
    return (conv_w[layer], alog, dtb, gdn_norm_w[layer][None, :], hgrn_norm_w[layer][None, :], hgrn_lb)


def _full_spec(a):
    nd = a.ndim
    return pl.BlockSpec(a.shape, lambda *_: (0,) * nd)


def _mix_prompt(projs, params, layer, batch, heads):
    n_rows = projs[0].shape[0]
    seq_rows = n_rows // batch
    seqs = PROMPT_SEQS_PER_STEP if batch % PROMPT_SEQS_PER_STEP == 0 else 1
    kw = heads * HEAD_DIM
    sel_np, lev_np, n_levels = _block_constants(1, CHUNK)
    sel = jnp.asarray(sel_np, BF16)
    lev = jnp.asarray(lev_np)
    row_map = lambda b, c: (b, c, 0)
    consts = params + (sel, lev)
    projs = [p.reshape(batch, seq_rows, p.shape[1]) for p in projs]
    kern = functools.partial(_mix_prompt_kernel, heads=heads, layer=layer, n_levels=n_levels, seqs=seqs)
    mix, conv_new, gdn_new, hg_new = pl.pallas_call(
        kern,
        grid=(batch // seqs, seq_rows // CHUNK),
        in_specs=[pl.BlockSpec((seqs, CHUNK, p.shape[2]), row_map) for p in projs]
                 + [_full_spec(a) for a in consts],
        out_specs=[pl.BlockSpec((seqs, CHUNK, 2 * kw), row_map),
                   pl.BlockSpec((seqs, CONV_W - 1, 3 * kw), lambda b, c: (b, 0, 0)),
                   pl.BlockSpec((seqs, heads, HEAD_DIM, HEAD_DIM), lambda b, c: (b, 0, 0, 0)),
                   pl.BlockSpec((seqs, heads, HEAD_DIM, HEAD_DIM), lambda b, c: (b, 0, 0, 0))],
        out_shape=[jax.ShapeDtypeStruct((batch, seq_rows, 2 * kw), BF16),
                   jax.ShapeDtypeStruct((batch, CONV_W - 1, 3 * kw), F32),
                   jax.ShapeDtypeStruct((batch, heads, HEAD_DIM, HEAD_DIM), F32),
                   jax.ShapeDtypeStruct((batch, heads, HEAD_DIM, HEAD_DIM), F32)],
        scratch_shapes=[pltpu.VMEM((seqs, 8 + CHUNK, 3 * kw), F32),
                        pltpu.VMEM((seqs, heads, HEAD_DIM, HEAD_DIM), F32),
                        pltpu.VMEM((seqs, heads, HEAD_DIM, HEAD_DIM), F32)],
        compiler_params=_cparams(("arbitrary", "arbitrary")),
        name=f"mix_prompt_l{layer}",
    )(*projs, *consts)
    return mix.reshape(n_rows, 2 * kw), conv_new, gdn_new, hg_new


def _mix_sample(projs, conv_state, gdn_state, hgrn_state, params, layer, n_seq, toks, heads, carried):
    kw = heads * HEAD_DIM
    seqs = SAMPLE_SEQS_PER_STEP
    sel_np, lev_np, n_levels = _block_constants(heads, toks)
    sel = jnp.asarray(sel_np, BF16)
    lev = jnp.asarray(lev_np)
    consts = params + (sel, lev)
    row_map = lambda n: (n, 0)
    kern = functools.partial(_mix_sample_kernel, heads=heads, layer=layer, n_levels=n_levels, toks=toks,
                             seqs=seqs)
    state_blk = pl.BlockSpec((1, seqs, heads, HEAD_DIM, HEAD_DIM), lambda n: (layer, n, 0, 0, 0))
    in_specs = [pl.BlockSpec((seqs * toks, p.shape[1]), row_map) for p in projs] + [
        pl.BlockSpec((1, seqs, CONV_W - 1, 3 * kw), lambda n: (layer, n, 0, 0)),
        state_blk, state_blk] + [_full_spec(a) for a in consts]
    args = tuple(projs) + (conv_state, gdn_state, hgrn_state) + consts
    aliases = {}
    if carried is not None:
        aliases = {len(args): 2, len(args) + 1: 3}
        in_specs += [pl.BlockSpec(memory_space=pl.ANY), pl.BlockSpec(memory_space=pl.ANY)]
        args += tuple(carried)
    return pl.pallas_call(
        kern,
        grid=(n_seq // seqs,),
        in_specs=in_specs,
        out_specs=[pl.BlockSpec((seqs * toks, 2 * kw), row_map),
                   pl.BlockSpec((seqs, CONV_W - 1, 3 * kw), lambda n: (n, 0, 0)),
                   state_blk, state_blk],
        out_shape=[jax.ShapeDtypeStruct((n_seq * toks, 2 * kw), BF16),
                   jax.ShapeDtypeStruct((n_seq, CONV_W - 1, 3 * kw), F32),
                   jax.ShapeDtypeStruct(gdn_state.shape, F32),
                   jax.ShapeDtypeStruct(hgrn_state.shape, F32)],
        scratch_shapes=[pltpu.VMEM((seqs, 8 + toks, 3 * kw), F32)],
        input_output_aliases=aliases,
        compiler_params=_cparams(("arbitrary",)),
        name=f"mix_sample_l{layer}",
    )(*args)


def _rmsnorm_rows(x, w):
    return x * lax.rsqrt(jnp.mean(x * x, axis=-1, keepdims=True) + EPS) * w


def _norm_kernel(x_ref, w_ref, o_ref):
    o_ref[...] = _rmsnorm_rows(x_ref[...], w_ref[...]).astype(o_ref.dtype)


def _norm(x, w, out_dtype, block_rows, grid, in_map, out_rows, out_map, name):
    d = x.shape[1]
    return pl.pallas_call(
        _norm_kernel,
        grid=grid,
        in_specs=[pl.BlockSpec((block_rows, d), in_map), pl.BlockSpec((1, d), lambda *_: (0, 0))],
        out_specs=pl.BlockSpec((block_rows, d), out_map),
        out_shape=jax.ShapeDtypeStruct((out_rows, d), out_dtype),
        compiler_params=_cparams(("arbitrary",) * len(grid)),
        name=name,
    )(x, w[None, :])


def _norm_drop_lead(x, w, batch, lead, seq, out_dtype, name):
    d = x.shape[1]
    tr = _pick_tile(seq, 512)
    return pl.pallas_call(
        _norm_kernel,
        grid=(batch, seq // tr),
        in_specs=[pl.BlockSpec((pl.Element(1), pl.Element(tr), pl.Element(d)),
                               lambda b, r: (b, pl.multiple_of(lead + r * tr, 8), 0)),
                  pl.BlockSpec((1, d), lambda b, r: (0, 0))],
        out_specs=pl.BlockSpec((1, tr, d), lambda b, r: (b, r, 0)),
        out_shape=jax.ShapeDtypeStruct((batch, seq, d), out_dtype),
        compiler_params=_cparams(("arbitrary", "arbitrary")),
        name=name,
    )(x.reshape(batch, lead + seq, d), w[None, :])


def _proj_kernel(h_ref, w_ref, o_ref):
    o_ref[...] = jnp.dot(h_ref[...], w_ref[...].astype(BF16), preferred_element_type=F32)


def _proj(h, w, layer, n, name):
    m, d = h.shape
    tm = _pick_tile(m, 2112)
    tn = _pick_tile(n, 1024 if w.dtype == BF16 else 512, HEAD_DIM)
    return pl.pallas_call(
        _proj_kernel,
        grid=(m // tm, n // tn),
        in_specs=[pl.BlockSpec((tm, d), lambda i, j: (i, 0)),
                  pl.BlockSpec((None, d, tn), lambda i, j: (layer, 0, j))],
        out_specs=pl.BlockSpec((tm, tn), lambda i, j: (i, j)),
        out_shape=jax.ShapeDtypeStruct((m, n), F32),
        compiler_params=_cparams(("arbitrary", "arbitrary")),
        name=name,
    )(h, w)


def _wout_kernel(x_ref, mix_ref, w_ref, nw_ref, x1_ref, h_ref):
    x1 = x_ref[...] + jnp.dot(mix_ref[...], w_ref[...], preferred_element_type=F32)
    x1_ref[...] = x1
    h_ref[...] = _rmsnorm_rows(x1, nw_ref[...]).astype(h_ref.dtype)


def _wout(x, mix, w_out, layer, norm_w, name):
    m, d = x.shape
    tm = _pick_tile(m, 352)
    return pl.pallas_call(
        _wout_kernel,
        grid=(m // tm,),
        in_specs=[pl.BlockSpec((tm, d), lambda i: (i, 0)),
                  pl.BlockSpec((tm, mix.shape[1]), lambda i: (i, 0)),
                  pl.BlockSpec((None,) + w_out.shape[1:], lambda i: (layer, 0, 0)),
                  pl.BlockSpec((1, d), lambda i: (0, 0))],
        out_specs=[pl.BlockSpec((tm, d), lambda i: (i, 0)), pl.BlockSpec((tm, d), lambda i: (i, 0))],
        out_shape=[jax.ShapeDtypeStruct((m, d), F32), jax.ShapeDtypeStruct((m, d), BF16)],
        compiler_params=_cparams(("arbitrary",)),
        name=name,
    )(x, mix, w_out, norm_w[None, :])


def _mlp_kernel(x_ref, h_ref, wup_ref, wdn_ref, nw_ref, xo_ref, ho_ref):
    f = pl.program_id(1)

    @pl.when(f == 0)
    def _():
        xo_ref[...] = x_ref[...]

    u = jnp.maximum(jnp.dot(h_ref[...], wup_ref[...], preferred_element_type=F32), 0.0)
    xo_ref[...] += jnp.dot((u * u).astype(BF16), wdn_ref[...], preferred_element_type=F32)

    @pl.when(f == pl.num_programs(1) - 1)
    def _():
        ho_ref[...] = _rmsnorm_rows(xo_ref[...], nw_ref[...]).astype(ho_ref.dtype)


def _mlp(x, h, w_up, w_down, layer, next_norm_w, name):
    m, d = x.shape
    ff = w_up.shape[2]
    tm = _pick_tile(m, 704)
    tf = _pick_tile(ff, 512, HEAD_DIM)
    return pl.pallas_call(
        _mlp_kernel,
        grid=(m // tm, ff // tf),
        in_specs=[pl.BlockSpec((tm, d), lambda i, f: (i, 0)),
                  pl.BlockSpec((tm, d), lambda i, f: (i, 0)),
                  pl.BlockSpec((None, d, tf), lambda i, f: (layer, 0, f)),
                  pl.BlockSpec((None, tf, d), lambda i, f: (layer, f, 0)),
                  pl.BlockSpec((1, d), lambda i, f: (0, 0))],
        out_specs=[pl.BlockSpec((tm, d), lambda i, f: (i, 0)), pl.BlockSpec((tm, d), lambda i, f: (i, 0))],
        out_shape=[jax.ShapeDtypeStruct((m, d), F32), jax.ShapeDtypeStruct((m, d), BF16)],
        compiler_params=_cparams(("arbitrary", "arbitrary")),
        name=name,
    )(x, h, w_up, w_down, next_norm_w[None, :])


def kernel(x_prompt, x_sample, state_gdn_conv, state_gdn, state_hgrn, meta_tokens, norm1_w, w_in, conv_w,
           a_log, dt_bias, gdn_norm_w, hgrn_lb, hgrn_norm_w, w_out, norm2_w, w_up, w_down, final_norm_w):
    batch, seq, d = x_prompt.shape
    n_seq, toks, _ = x_sample.shape
    depth = w_in.shape[0]
    heads = a_log.shape[1]
    kw = heads * HEAD_DIM
    n_meta = meta_tokens.shape[0]
    assert seq % CHUNK == 0 and 0 < n_meta <= CHUNK and n_seq % SAMPLE_SEQS_PER_STEP == 0
    assert w_in.shape[2] == 8 * kw + 2 * heads and hgrn_lb.shape[1] == kw and 2 * heads <= HEAD_DIM

    w_hg = w_in[:, :, 4 * kw + 2 * heads:].astype(BF16)
    w_ba = jnp.pad(w_in[:, :, 4 * kw:4 * kw + 2 * heads],
                   ((0, 0), (0, 0), (0, HEAD_DIM - 2 * heads))).astype(BF16)
    w_out_b = w_out.astype(BF16)
    w_up_b = w_up.astype(BF16)
    w_down_b = w_down.astype(BF16)

    lead = CHUNK - n_meta
    seq_rows = CHUNK + seq
    xp = jnp.concatenate([jnp.zeros((batch, lead, d), x_prompt.dtype),
                          jnp.broadcast_to(meta_tokens.astype(x_prompt.dtype)[None], (batch, n_meta, d)),
                          x_prompt], axis=1).reshape(batch * seq_rows, d)
    xs = x_sample.reshape(n_seq * toks, d)

    def first_norm(x, name):
        tm = _pick_tile(x.shape[0], 512)
        return _norm(x, norm1_w[0], BF16, tm, (x.shape[0] // tm,), lambda i: (i, 0), x.shape[0],
                     lambda i: (i, 0), name)

    hp = first_norm(xp, "norm_in_prompt")
    hs = first_norm(xs, "norm_in_sample")
    conv_p, gdn_p, hg_p, conv_s = [], [], [], []
    carried = None
    for l in range(depth):
        params = _mixer_params(l, heads, conv_w, a_log, dt_bias, gdn_norm_w, hgrn_norm_w, hgrn_lb)
        next_w = norm1_w[l + 1] if l + 1 < depth else final_norm_w
        projs = (_proj(hp, w_in, l, 4 * kw, f"proj_gdn_prompt_l{l}"),
                 _proj(hp, w_hg, l, 4 * kw, f"proj_hgrn_prompt_l{l}"),
                 _proj(hp, w_ba, l, HEAD_DIM, f"proj_ba_prompt_l{l}"))
        mix, c_new, g_new, h_new = _mix_prompt(projs, params, l, batch, heads)
        x1, h2 = _wout(xp, mix, w_out_b, l, norm2_w[l], f"wout_prompt_l{l}")
        xp, hp = _mlp(x1, h2, w_up_b, w_down_b, l, next_w, f"mlp_prompt_l{l}")
        conv_p.append(c_new); gdn_p.append(g_new); hg_p.append(h_new)
        projs = (_proj(hs, w_in, l, 4 * kw, f"proj_gdn_sample_l{l}"),
                 _proj(hs, w_hg, l, 4 * kw, f"proj_hgrn_sample_l{l}"),
                 _proj(hs, w_ba, l, HEAD_DIM, f"proj_ba_sample_l{l}"))
        mix, c_new, g_new, h_new = _mix_sample(projs, state_gdn_conv, state_gdn, state_hgrn, params, l,
                                               n_seq, toks, heads, carried)
        carried = (g_new, h_new)
        x1, h2 = _wout(xs, mix, w_out_b, l, norm2_w[l], f"wout_sample_l{l}")
        xs, hs = _mlp(x1, h2, w_up_b, w_down_b, l, next_w, f"mlp_sample_l{l}")
        conv_s.append(c_new)

    yp = _norm_drop_lead(xp, final_norm_w, batch, CHUNK, seq, x_prompt.dtype, "norm_out_prompt")
    tm = _pick_tile(n_seq * toks, 512)
    ys = _norm(xs, final_norm_w, x_sample.dtype, tm, (n_seq * toks // tm,), lambda i: (i, 0), n_seq * toks,
               lambda i: (i, 0), "norm_out_sample").reshape(n_seq, toks, d)
    return (yp, ys, jnp.stack(conv_p).astype(x_prompt.dtype), jnp.stack(gdn_p).astype(state_gdn.dtype),
            jnp.stack(hg_p).astype(state_hgrn.dtype), jnp.stack(conv_s).astype(state_gdn_conv.dtype),
            carried[0].astype(state_gdn.dtype), carried[1].astype(state_hgrn.dtype))
```

```python
import functools
import math

import numpy as np
import jax
import jax.numpy as jnp
from jax import lax
from jax.experimental import pallas as pl
from jax.experimental.pallas import tpu as pltpu

HEAD_DIM = 128
CHUNK = 64
CONV_W = 4
EPS = 1e-6
SAMPLE_SEQS_PER_STEP = 4
PROMPT_SEQS_PER_STEP = 2
V7X_VMEM_LIMIT_BYTES = 56 * 1024 * 1024

F32 = jnp.float32
BF16 = jnp.bfloat16


def _mm(a, b):
    return jnp.dot(a.astype(BF16), b.astype(BF16), preferred_element_type=F32)


def _mm_nt(a, b):
    return lax.dot_general(a.astype(BF16), b.astype(BF16), (((1,), (1,)), ((), ())),
                           preferred_element_type=F32)


def _mm_tn(a, b):
    return lax.dot_general(a.astype(BF16), b.astype(BF16), (((0,), (0,)), ((), ())),
                           preferred_element_type=F32)


def _split_bf16(x):
    hi = x.astype(BF16)
    return hi, (x - hi.astype(F32)).astype(BF16)


def _mm_3pass(a, b):
    ah, al = _split_bf16(a)
    bh, bl = _split_bf16(b)
    return (jnp.dot(ah, bh, preferred_element_type=F32) + jnp.dot(ah, bl, preferred_element_type=F32)
            + jnp.dot(al, bh, preferred_element_type=F32))


def _mm_sel(m01, x):
    x1 = x.astype(BF16)
    r1 = x - x1.astype(F32)
    x2 = r1.astype(BF16)
    x3 = (r1 - x2.astype(F32)).astype(BF16)
    return (jnp.dot(m01, x1, preferred_element_type=F32)
            + jnp.dot(m01, x2, preferred_element_type=F32)
            + jnp.dot(m01, x3, preferred_element_type=F32))


def _sigmoid(x):
    return 1.0 / (1.0 + jnp.exp(-x))


def _sigmoid_abs(x):
    return 0.5 * jnp.tanh(0.5 * x) + 0.5


def _silu(x):
    h = 0.5 * x
    return h + h * jnp.tanh(h)


def _softplus(x):
    return jnp.maximum(x, 0.0) + jnp.log(1.0 + jnp.exp(-jnp.abs(x)))


def _pick_tile(m, target, mult=16):
    best = None
    for t in range(mult, m + 1, mult):
        if m % t == 0 and t <= target:
            best = t
    if best is None:
        raise ValueError(f"no tile for {m}")
    return best


def _cparams(sem):
    return pltpu.CompilerParams(dimension_semantics=sem, vmem_limit_bytes=V7X_VMEM_LIMIT_BYTES)


@functools.lru_cache(maxsize=None)
def _block_constants(groups, toks):
    rows = groups * toks
    grp = np.arange(rows) // toks
    pos = np.arange(rows) % toks
    same = grp[:, None] == grp[None, :]
    pt, ps = pos[:, None], pos[None, :]
    prefix = (same & (ps <= pt)).astype(np.float32)
    lev = np.full((rows, rows), -1, np.int32)
    lev[np.arange(rows), np.arange(rows)] = 0
    n, idx = toks, 1
    while n >= 2:
        half = n // 2
        blk = same & ((pt // n) == (ps // n))
        lev[blk & (pt % n >= half) & (ps % n < half)] = idx
        n //= 2
        idx += 1
    return prefix, lev, idx - 1


def _rows_from(b, offset, n):
    rows = b.shape[0]
    if n % 8 == 0:
        return jnp.concatenate(
            [jnp.broadcast_to(b[s + offset:s + offset + 1, :], (n, HEAD_DIM)) for s in range(0, rows, n)], axis=0)
    b3 = b.reshape(rows // 8, 8, HEAD_DIM)
    sub = lax.broadcasted_iota(jnp.int32, b3.shape, 1)
    out = None
    for s in range(0, 8, n):
        cand = jnp.broadcast_to(b3[:, s + offset:s + offset + 1, :], b3.shape)
        out = cand if out is None else jnp.where(sub >= s, cand, out)
    return out.reshape(rows, HEAD_DIM)


def _gdn_blocks(qs, ks, vs, betas, bcols, brows, sufcols, lev, states, toks):
    blocks = range(len(qs))
    rows = qs[0].shape[0]
    groups = range(rows // toks)
    gsl = [slice(g * toks, (g + 1) * toks) for g in groups]
    eye = (lev == 0).astype(F32)
    decay = [jnp.exp(jnp.where(lev >= 0, bcols[p] - brows[p], -jnp.inf)) for p in blocks]
    kb = [ks[p] * betas[p] for p in blocks]
    a = [jnp.where(lev >= 1, _mm_nt(kb[p], ks[p]) * decay[p], 0.0) for p in blocks]
    eb = [jnp.exp(bcols[p]) for p in blocks]
    qe = [qs[p] * eb[p] for p in blocks]
    o_state = [jnp.concatenate([_mm(qe[p][gsl[g]], states[p][g]) for g in groups], axis=0) for p in blocks]
    qk = [_mm_nt(qs[p], ks[p]) * decay[p] for p in blocks]
    inv = [eye - a[p] for p in blocks]
    apow = a
    for _ in range(int(math.log2(toks)) - 1):
        apow = [_mm_3pass(x, x) for x in apow]
        inv = [inv[p] + _mm_3pass(inv[p], apow[p]) for p in blocks]
    sol = [_mm_3pass(inv[p], jnp.concatenate([vs[p] * betas[p], kb[p] * eb[p]], axis=1)) for p in blocks]
    v_new = [sol[p][:, :HEAD_DIM]
             - jnp.concatenate([_mm(sol[p][gsl[g], HEAD_DIM:], states[p][g]) for g in groups], axis=0)
             for p in blocks]
    o = [o_state[p] + _mm(qk[p], v_new[p]) for p in blocks]
    kdec = [ks[p] * jnp.exp(sufcols[p]) for p in blocks]
    new_states = []
    for p in blocks:
        per_group = []
        for g in groups:
            r0 = g * toks
            ebl = jnp.exp(bcols[p][r0:r0 + 1] + sufcols[p][r0:r0 + 1])
            per_group.append(ebl * states[p][g] + _mm_tn(kdec[p][gsl[g]], v_new[p][gsl[g]]))
        new_states.append(per_group)
    return o, new_states


def _hgrn_blocks(qs, f_logits, vs, lbs, prefix, lev, n_levels, states_t, toks):
    blocks = range(len(qs))
    rows = qs[0].shape[0]
    groups = range(rows // toks)
    gsl = [slice(g * toks, (g + 1) * toks) for g in groups]
    fg = [lbs[p] + (1.0 - lbs[p]) * _sigmoid(f_logits[p]) for p in blocks]
    ks = [1.0 - fg[p] for p in blocks]
    b = [_mm_sel(prefix, jnp.log(fg[p])) for p in blocks]
    x_pre = [jnp.exp(b[p]) for p in blocks]
    x_suf = [jnp.exp(-jnp.abs(b[p] - _rows_from(b[p], toks - 1, toks))) for p in blocks]
    a = [jnp.where(lev == 0, _mm_nt(qs[p], ks[p]), 0.0) for p in blocks]
    for i in range(n_levels):
        n = toks >> i
        xl = [jnp.exp(-jnp.abs(b[p] - _rows_from(b[p], n // 2 - 1, n))) for p in blocks]
        a = [a[p] + jnp.where(lev == i + 1, _mm_nt(qs[p] * xl[p], ks[p] * xl[p]), 0.0) for p in blocks]
    o = [jnp.concatenate([_mm_nt((qs[p] * x_pre[p])[gsl[g]], states_t[p][g]) for g in groups], axis=0)
         + _mm(a[p], vs[p]) for p in blocks]
    new_states = []
    for p in blocks:
        kdec = ks[p] * x_suf[p]
        per_group = []
        for g in groups:
            last = g * toks + toks - 1
            per_group.append(states_t[p][g] * x_pre[p][last:last + 1] + _mm_tn(vs[p][gsl[g]], kdec[gsl[g]]))
        new_states.append(per_group)
    return o, new_states


def _head_norm_gate(o, z, w):
    y = o * lax.rsqrt(jnp.mean(o * o, axis=-1, keepdims=True) + EPS)
    return y * w * _silu(z)


def _l2norm(x):
    return x * lax.rsqrt(jnp.sum(x * x, axis=-1, keepdims=True) + EPS)


def _layer_lower_bound(lbp, layer):
    m = jnp.max(lbp, axis=0, keepdims=True)
    e = jnp.exp(lbp - m)
    p = e / jnp.sum(e, axis=0, keepdims=True)
    if layer == 0:
        return jnp.zeros_like(m)
    return jnp.sum(p[1:layer + 1], axis=0, keepdims=True)


def _conv_slab(cbuf, base, rows, cols, w):
    y = (w[3:4] * cbuf[base:base + rows, cols] + w[2:3] * cbuf[base - 1:base - 1 + rows, cols]
         + w[1:2] * cbuf[base - 2:base - 2 + rows, cols] + w[0:1] * cbuf[base - 3:base - 3 + rows, cols])
    return _silu(y)


def _mix_prompt_kernel(pg_ref, ph_ref, ba_ref, convw_ref, alog_ref, dtb_ref, gnw_ref, hnw_ref, lbp_ref,
                       sel_ref, lev_ref, mix_ref, convo_ref, sgo_ref, sho_ref,
                       cbuf, sg, sht, *, heads, layer, n_levels, seqs):
    c = pl.program_id(1)
    kw = heads * HEAD_DIM
    rows = CHUNK

    @pl.when(c == 0)
    def _():
        cbuf[:, 0:8, :] = jnp.zeros((seqs, 8, 3 * kw), F32)
        sg[...] = jnp.zeros_like(sg)
        sht[...] = jnp.zeros_like(sht)

    sel = sel_ref[...]
    lev = lev_ref[...]
    gnw = gnw_ref[...]
    hnw = hnw_ref[...]
    scale = HEAD_DIM ** -0.5
    ps = [(s, h) for s in range(seqs) for h in range(heads)]

    def slab(off, h):
        return slice(off + h * HEAD_DIM, off + (h + 1) * HEAD_DIM)

    def conv(s, off, h):
        return _conv_slab(cbuf.at[s], 8, rows, slab(off, h), convw_ref[:, slab(off, h)])

    for s in range(seqs):
        cbuf[s, 8:8 + rows, :] = pg_ref[s, :, 0:3 * kw]
    qs = [_l2norm(conv(s, 0, h)) * scale for s, h in ps]
    ks = [_l2norm(conv(s, kw, h)) for s, h in ps]
    vs = [conv(s, 2 * kw, h) for s, h in ps]
    for s in range(seqs):
        tail = cbuf[s, 8 + rows - 3:8 + rows, :]
        cbuf[s, 5:8, :] = tail
        convo_ref[s] = tail

    beta_all, b_all, b_all_t, suf_all = [], [], [], []
    for s in range(seqs):
        ba = ba_ref[s]
        beta_all.append(_sigmoid_abs(ba))
        g_all = -jnp.exp(alog_ref[...]) * _softplus(ba + dtb_ref[...])
        b = _mm_sel(sel, g_all)
        b_all.append(b)
        b_all_t.append(b.T)
        suf_all.append(b[rows - 1:rows, :] - b)
    os_, new = _gdn_blocks(qs, ks, vs, [beta_all[s][:, h:h + 1] for s, h in ps],
                           [b_all[s][:, heads + h:heads + h + 1] for s, h in ps],
                           [b_all_t[s][heads + h:heads + h + 1, :] for s, h in ps],
                           [suf_all[s][:, heads + h:heads + h + 1] for s, h in ps], lev,
                           [[sg[s, h]] for s, h in ps], rows)
    for p, (s, h) in enumerate(ps):
        sg[s, h] = new[p][0]
        mix_ref[s, :, slab(0, h)] = _head_norm_gate(os_[p], pg_ref[s, :, slab(3 * kw, h)],
                                                    gnw).astype(mix_ref.dtype)

    lb_all = _layer_lower_bound(lbp_ref[...], layer)
    os_, new = _hgrn_blocks([ph_ref[s, :, slab(0, h)] for s, h in ps],
                            [ph_ref[s, :, slab(kw, h)] for s, h in ps],
                            [ph_ref[s, :, slab(2 * kw, h)] for s, h in ps],
                            [lb_all[:, slab(0, h)] for s, h in ps], sel, lev, n_levels,
                            [[sht[s, h]] for s, h in ps], rows)
    for p, (s, h) in enumerate(ps):
        sht[s, h] = new[p][0]
        mix_ref[s, :, slab(kw, h)] = _head_norm_gate(os_[p], ph_ref[s, :, slab(3 * kw, h)],
                                                     hnw).astype(mix_ref.dtype)

    @pl.when(c == pl.num_programs(1) - 1)
    def _():
        sgo_ref[...] = sg[...]
        for s, h in ps:
            sho_ref[s, h] = sht[s, h].T


def _mix_sample_kernel(pg_ref, ph_ref, ba_ref, convs_ref, sgi_ref, shi_ref, convw_ref, alog_ref, dtb_ref,
                       gnw_ref, hnw_ref, lbp_ref, sel_ref, lev_ref, *rest,
                       heads, layer, n_levels, toks, seqs):
    mix_ref, convo_ref, sgo_ref, sho_ref, cbuf = rest[-5:]
    kw = heads * HEAD_DIM
    sel = sel_ref[...]
    lev = lev_ref[...]
    rows = heads * toks
    gnw = gnw_ref[...]
    hnw = hnw_ref[...]
    scale = HEAD_DIM ** -0.5
    lb_all = _layer_lower_bound(lbp_ref[...], layer)
    lb_rows = jnp.concatenate(
        [jnp.broadcast_to(lb_all[:, h * HEAD_DIM:(h + 1) * HEAD_DIM], (toks, HEAD_DIM)) for h in range(heads)],
        axis=0)

    ss = range(seqs)
    hs = range(heads)
    trs = [slice(s * toks, (s + 1) * toks) for s in ss]

    def slab(off, h):
        return slice(off + h * HEAD_DIM, off + (h + 1) * HEAD_DIM)

    def stack(ref, s, off):
        return jnp.concatenate([ref[trs[s], slab(off, h)] for h in hs], axis=0)

    def conv_stack(s, off):
        return jnp.concatenate(
            [_conv_slab(cbuf.at[s], 8, toks, slab(off, h), convw_ref[:, slab(off, h)]) for h in hs], axis=0)

    for s in ss:
        cbuf[s, 5:8, :] = convs_ref[0, s]
        cbuf[s, 8:8 + toks, :] = pg_ref[trs[s], 0:3 * kw]
    qs = [_l2norm(conv_stack(s, 0)) * scale for s in ss]
    ks = [_l2norm(conv_stack(s, kw)) for s in ss]
    vs = [conv_stack(s, 2 * kw) for s in ss]
    for s in ss:
        convo_ref[s] = cbuf[s, 8 + toks - 3:8 + toks, :]

    betas, bcols, brows, sufcols = [], [], [], []
    for s in ss:
        ba = ba_ref[trs[s], :]
        beta_all = _sigmoid_abs(ba)
        g_all = -jnp.exp(alog_ref[...]) * _softplus(ba + dtb_ref[...])
        betas.append(jnp.concatenate([beta_all[:, h:h + 1] for h in hs], axis=0))
        gcol = jnp.concatenate([g_all[:, heads + h:heads + h + 1] for h in hs], axis=0)
        b_b = _mm_sel(sel, jnp.broadcast_to(gcol, (rows, HEAD_DIM)))
        bcols.append(b_b[:, 0:1])
        brows.append(b_b.T[0:1, :])
        sufcols.append((_rows_from(b_b, toks - 1, toks) - b_b)[:, 0:1])
    os_, new = _gdn_blocks(qs, ks, vs, betas, bcols, brows, sufcols, lev,
                           [[sgi_ref[0, s, h] for h in hs] for s in ss], toks)
    for s in ss:
        og = _head_norm_gate(os_[s], stack(pg_ref, s, 3 * kw), gnw)
        for h in hs:
            sgo_ref[0, s, h] = new[s][h]
            mix_ref[trs[s], slab(0, h)] = og[h * toks:(h + 1) * toks].astype(mix_ref.dtype)

    os_, new = _hgrn_blocks([stack(ph_ref, s, 0) for s in ss], [stack(ph_ref, s, kw) for s in ss],
                            [stack(ph_ref, s, 2 * kw) for s in ss], [lb_rows for s in ss], sel, lev, n_levels,
                            [[shi_ref[0, s, h].T for h in hs] for s in ss], toks)
    for s in ss:
        oh = _head_norm_gate(os_[s], stack(ph_ref, s, 3 * kw), hnw)
        for h in hs:
            sho_ref[0, s, h] = new[s][h].T
            mix_ref[trs[s], slab(kw, h)] = oh[h * toks:(h + 1) * toks].astype(mix_ref.dtype)


def _mixer_params(layer, heads, conv_w, a_log, dt_bias, gdn_norm_w, hgrn_norm_w, hgrn_lb):
    pad = HEAD_DIM - 2 * heads
    alog = jnp.pad(a_log[layer][None, :], ((0, 0), (heads, pad)))
    dtb = jnp.pad(dt_bias[layer][None, :], ((0, 0), (heads, pad)))
    return (conv_w[layer], alog, dtb, gdn_norm_w[layer][None, :], hgrn_norm_w[layer][None, :], hgrn_lb)


def _full_spec(a):
    nd = a.ndim
    return pl.BlockSpec(a.shape, lambda *_: (0,) * nd)


def _mix_prompt(projs, params, layer, batch, heads):
    n_rows = projs[0].shape[0]
    seq_rows = n_rows // batch
    seqs = PROMPT_SEQS_PER_STEP if batch % PROMPT_SEQS_PER_STEP == 0 else 1
    kw = heads * HEAD_DIM
    sel_np, lev_np, n_levels = _block_constants(1, CHUNK)
    sel = jnp.asarray(sel_np, BF16)
    lev = jnp.asarray(lev_np)
    row_map = lambda b, c: (b, c, 0)
    consts = params + (sel, lev)
    projs = [p.reshape(batch, seq_rows, p.shape[1]) for p in projs]
    kern = functools.partial(_mix_prompt_kernel, heads=heads, layer=layer, n_levels=n_levels, seqs=seqs)
    mix, conv_new, gdn_new, hg_new = pl.pallas_call(
        kern,
        grid=(batch // seqs, seq_rows // CHUNK),
        in_specs=[pl.BlockSpec((seqs, CHUNK, p.shape[2]), row_map) for p in projs]
                 + [_full_spec(a) for a in consts],
        out_specs=[pl.BlockSpec((seqs, CHUNK, 2 * kw), row_map),
                   pl.BlockSpec((seqs, CONV_W - 1, 3 * kw), lambda b, c: (b, 0, 0)),
                   pl.BlockSpec((seqs, heads, HEAD_DIM, HEAD_DIM), lambda b, c: (b, 0, 0, 0)),
                   pl.BlockSpec((seqs, heads, HEAD_DIM, HEAD_DIM), lambda b, c: (b, 0, 0, 0))],
        out_shape=[jax.ShapeDtypeStruct((batch, seq_rows, 2 * kw), BF16),
                   jax.ShapeDtypeStruct((batch, CONV_W - 1, 3 * kw), F32),
                   jax.ShapeDtypeStruct((batch, heads, HEAD_DIM, HEAD_DIM), F32),
                   jax.ShapeDtypeStruct((batch, heads, HEAD_DIM, HEAD_DIM), F32)],
        scratch_shapes=[pltpu.VMEM((seqs, 8 + CHUNK, 3 * kw), F32),
                        pltpu.VMEM((seqs, heads, HEAD_DIM, HEAD_DIM), F32),
                        pltpu.VMEM((seqs, heads, HEAD_DIM, HEAD_DIM), F32)],
        compiler_params=_cparams(("arbitrary", "arbitrary")),
        name=f"mix_prompt_l{layer}",
    )(*projs, *consts)
    return mix.reshape(n_rows, 2 * kw), conv_new, gdn_new, hg_new


def _mix_sample(projs, conv_state, gdn_state, hgrn_state, params, layer, n_seq, toks, heads, carried):
    kw = heads * HEAD_DIM
    seqs = SAMPLE_SEQS_PER_STEP
    sel_np, lev_np, n_levels = _block_constants(heads, toks)
    sel = jnp.asarray(sel_np, BF16)
    lev = jnp.asarray(lev_np)
    consts = params + (sel, lev)
    row_map = lambda n: (n, 0)
    kern = functools.partial(_mix_sample_kernel, heads=heads, layer=layer, n_levels=n_levels, toks=toks,
                             seqs=seqs)
    state_blk = pl.BlockSpec((1, seqs, heads, HEAD_DIM, HEAD_DIM), lambda n: (layer, n, 0, 0, 0))
    in_specs = [pl.BlockSpec((seqs * toks, p.shape[1]), row_map) for p in projs] + [
        pl.BlockSpec((1, seqs, CONV_W - 1, 3 * kw), lambda n: (layer, n, 0, 0)),
        state_blk, state_blk] + [_full_spec(a) for a in consts]
    args = tuple(projs) + (conv_state, gdn_state, hgrn_state) + consts
    aliases = {len(args): 2, len(args) + 1: 3}
    in_specs += [pl.BlockSpec(memory_space=pl.ANY), pl.BlockSpec(memory_space=pl.ANY)]
    args += tuple(carried)
    return pl.pallas_call(
        kern,
        grid=(n_seq // seqs,),
        in_specs=in_specs,
        out_specs=[pl.BlockSpec((seqs * toks, 2 * kw), row_map),
                   pl.BlockSpec((seqs, CONV_W - 1, 3 * kw), lambda n: (n, 0, 0)),
                   state_blk, state_blk],
        out_shape=[jax.ShapeDtypeStruct((n_seq * toks, 2 * kw), BF16),
                   jax.ShapeDtypeStruct((n_seq, CONV_W - 1, 3 * kw), F32),
                   jax.ShapeDtypeStruct(gdn_state.shape, F32),
                   jax.ShapeDtypeStruct(hgrn_state.shape, F32)],
        scratch_shapes=[pltpu.VMEM((seqs, 8 + toks, 3 * kw), F32)],
        input_output_aliases=aliases,
        compiler_params=_cparams(("arbitrary",)),
        name=f"mix_sample_l{layer}",
    )(*args)


def _rmsnorm_rows(x, w):
    return x * lax.rsqrt(jnp.mean(x * x, axis=-1, keepdims=True) + EPS) * w


def _norm_kernel(x_ref, w_ref, o_ref):
    o_ref[...] = _rmsnorm_rows(x_ref[...], w_ref[...]).astype(o_ref.dtype)


def _norm(x, w, out_dtype, block_rows, grid, in_map, out_rows, out_map, name):
    d = x.shape[1]
    return pl.pallas_call(
        _norm_kernel,
        grid=grid,
        in_specs=[pl.BlockSpec((block_rows, d), in_map), pl.BlockSpec((1, d), lambda *_: (0, 0))],
        out_specs=pl.BlockSpec((block_rows, d), out_map),
        out_shape=jax.ShapeDtypeStruct((out_rows, d), out_dtype),
        compiler_params=_cparams(("arbitrary",) * len(grid)),
        name=name,
    )(x, w[None, :])


def _norm_drop_lead(x, w, batch, lead, seq, out_dtype, name):
    d = x.shape[1]
    tr = _pick_tile(seq, 512)
    return pl.pallas_call(
        _norm_kernel,
        grid=(batch, seq // tr),
        in_specs=[pl.BlockSpec((pl.Element(1), pl.Element(tr), pl.Element(d)),
                               lambda b, r: (b, pl.multiple_of(lead + r * tr, 8), 0)),
                  pl.BlockSpec((1, d), lambda b, r: (0, 0))],
        out_specs=pl.BlockSpec((1, tr, d), lambda b, r: (b, r, 0)),
        out_shape=jax.ShapeDtypeStruct((batch, seq, d), out_dtype),
        compiler_params=_cparams(("arbitrary", "arbitrary")),
        name=name,
    )(x.reshape(batch, lead + seq, d), w[None, :])


def _proj_kernel(h_ref, wt_ref, o_ref):
    o_ref[...] = lax.dot_general(h_ref[...], wt_ref[0].astype(BF16), (((1,), (1,)), ((), ())),
                                 preferred_element_type=F32)


def _proj(h, w_t, layer, row0, n, name):
    m, d = h.shape
    tm = _pick_tile(m, 2112)
    tn = _pick_tile(n, 512, HEAD_DIM)
    assert row0 % 8 == 0
    return pl.pallas_call(
        _proj_kernel,
        grid=(m // tm, n // tn),
        in_specs=[pl.BlockSpec((tm, d), lambda i, j: (i, 0)),
                  pl.BlockSpec((pl.Element(1), pl.Element(tn), pl.Element(d)),
                               lambda i, j: (layer, pl.multiple_of(row0 + j * tn, 8), 0))],
        out_specs=pl.BlockSpec((tm, tn), lambda i, j: (i, j)),
        out_shape=jax.ShapeDtypeStruct((m, n), F32),
        compiler_params=_cparams(("arbitrary", "arbitrary")),
        name=name,
    )(h, w_t)


def _wout_kernel(x_ref, mix_ref, w_ref, nw_ref, x1_ref, h_ref):
    x1 = x_ref[...] + jnp.dot(mix_ref[...], w_ref[...], preferred_element_type=F32)
    x1_ref[...] = x1
    h_ref[...] = _rmsnorm_rows(x1, nw_ref[...]).astype(h_ref.dtype)


def _wout(x, mix, w_out, layer, norm_w, name):
    m, d = x.shape
    tm = _pick_tile(m, 352)
    return pl.pallas_call(
        _wout_kernel,
        grid=(m // tm,),
        in_specs=[pl.BlockSpec((tm, d), lambda i: (i, 0)),
                  pl.BlockSpec((tm, mix.shape[1]), lambda i: (i, 0)),
                  pl.BlockSpec((None,) + w_out.shape[1:], lambda i: (layer, 0, 0)),
                  pl.BlockSpec((1, d), lambda i: (0, 0))],
        out_specs=[pl.BlockSpec((tm, d), lambda i: (i, 0)), pl.BlockSpec((tm, d), lambda i: (i, 0))],
        out_shape=[jax.ShapeDtypeStruct((m, d), F32), jax.ShapeDtypeStruct((m, d), BF16)],
        compiler_params=_cparams(("arbitrary",)),
        name=name,
    )(x, mix, w_out, norm_w[None, :])


def _mlp_kernel(x_ref, h_ref, wup_ref, wdn_ref, nw_ref, xo_ref, ho_ref):
    f = pl.program_id(1)

    @pl.when(f == 0)
    def _():
        xo_ref[...] = x_ref[...]

    u = jnp.maximum(jnp.dot(h_ref[...], wup_ref[...], preferred_element_type=F32), 0.0)
    xo_ref[...] += jnp.dot((u * u).astype(BF16), wdn_ref[...], preferred_element_type=F32)

    @pl.when(f == pl.num_programs(1) - 1)
    def _():
        ho_ref[...] = _rmsnorm_rows(xo_ref[...], nw_ref[...]).astype(ho_ref.dtype)


def _mlp(x, h, w_up, w_down, layer, next_norm_w, name):
    m, d = x.shape
    ff = w_up.shape[2]
    tm = _pick_tile(m, 704)
    tf = _pick_tile(ff, 512, HEAD_DIM)
    return pl.pallas_call(
        _mlp_kernel,
        grid=(m // tm, ff // tf),
        in_specs=[pl.BlockSpec((tm, d), lambda i, f: (i, 0)),
                  pl.BlockSpec((tm, d), lambda i, f: (i, 0)),
                  pl.BlockSpec((None, d, tf), lambda i, f: (layer, 0, f)),
                  pl.BlockSpec((None, tf, d), lambda i, f: (layer, f, 0)),
                  pl.BlockSpec((1, d), lambda i, f: (0, 0))],
        out_specs=[pl.BlockSpec((tm, d), lambda i, f: (i, 0)), pl.BlockSpec((tm, d), lambda i, f: (i, 0))],
        out_shape=[jax.ShapeDtypeStruct((m, d), F32), jax.ShapeDtypeStruct((m, d), BF16)],
        compiler_params=_cparams(("arbitrary", "arbitrary")),
        name=name,
    )(x, h, w_up, w_down, next_norm_w[None, :])


def kernel(x_prompt, x_sample, state_gdn_conv, state_gdn, state_hgrn, meta_tokens, norm1_w, w_in, conv_w,
           a_log, dt_bias, gdn_norm_w, hgrn_lb, hgrn_norm_w, w_out, norm2_w, w_up, w_down, final_norm_w):
    batch, seq, d = x_prompt.shape
    n_seq, toks, _ = x_sample.shape
    depth = w_in.shape[0]
    heads = a_log.shape[1]
    kw = heads * HEAD_DIM
    n_meta = meta_tokens.shape[0]
    assert seq % CHUNK == 0 and 0 < n_meta <= CHUNK and n_seq % SAMPLE_SEQS_PER_STEP == 0
    assert w_in.shape[2] == 8 * kw + 2 * heads and hgrn_lb.shape[1] == kw and 2 * heads <= HEAD_DIM

    w_in_t = jnp.swapaxes(w_in, 1, 2)
    w_ba_t = jnp.pad(w_in_t[:, 4 * kw:4 * kw + 2 * heads, :],
                     ((0, 0), (0, HEAD_DIM - 2 * heads), (0, 0))).astype(BF16)
    w_out_b = w_out.astype(BF16)
    w_up_b = w_up.astype(BF16)
    w_down_b = w_down.astype(BF16)

    lead = CHUNK - n_meta
    seq_rows = CHUNK + seq
    xp = jnp.concatenate([jnp.zeros((batch, lead, d), x_prompt.dtype),
                          jnp.broadcast_to(meta_tokens.astype(x_prompt.dtype)[None], (batch, n_meta, d)),
                          x_prompt], axis=1).reshape(batch * seq_rows, d)
    xs = x_sample.reshape(n_seq * toks, d)

    def first_norm(x, name):
        tm = _pick_tile(x.shape[0], 512)
        return _norm(x, norm1_w[0], BF16, tm, (x.shape[0] // tm,), lambda i: (i, 0), x.shape[0],
                     lambda i: (i, 0), name)

    hp = first_norm(xp, "norm_in_prompt")
    hs = first_norm(xs, "norm_in_sample")
    conv_p, gdn_p, hg_p, conv_s = [], [], [], []
    carried = (jnp.zeros(state_gdn.shape, F32), jnp.zeros(state_hgrn.shape, F32))
    for l in range(depth):
        params = _mixer_params(l, heads, conv_w, a_log, dt_bias, gdn_norm_w, hgrn_norm_w, hgrn_lb)
        next_w = norm1_w[l + 1] if l + 1 < depth else final_norm_w
        projs = (_proj(hp, w_in_t, l, 0, 4 * kw, f"proj_gdn_prompt_l{l}"),
                 _proj(hp, w_in_t, l, 4 * kw + 2 * heads, 4 * kw, f"proj_hgrn_prompt_l{l}"),
                 _proj(hp, w_ba_t, l, 0, HEAD_DIM, f"proj_ba_prompt_l{l}"))
        mix, c_new, g_new, h_new = _mix_prompt(projs, params, l, batch, heads)
        x1, h2 = _wout(xp, mix, w_out_b, l, norm2_w[l], f"wout_prompt_l{l}")
        xp, hp = _mlp(x1, h2, w_up_b, w_down_b, l, next_w, f"mlp_prompt_l{l}")
        conv_p.append(c_new); gdn_p.append(g_new); hg_p.append(h_new)
        projs = (_proj(hs, w_in_t, l, 0, 4 * kw, f"proj_gdn_sample_l{l}"),
                 _proj(hs, w_in_t, l, 4 * kw + 2 * heads, 4 * kw, f"proj_hgrn_sample_l{l}"),
                 _proj(hs, w_ba_t, l, 0, HEAD_DIM, f"proj_ba_sample_l{l}"))
        mix, c_new, g_new, h_new = _mix_sample(projs, state_gdn_conv, state_gdn, state_hgrn, params, l,
                                               n_seq, toks, heads, carried)
        carried = (g_new, h_new)
        x1, h2 = _wout(xs, mix, w_out_b, l, norm2_w[l], f"wout_sample_l{l}")
        xs, hs = _mlp(x1, h2, w_up_b, w_down_b, l, next_w, f"mlp_sample_l{l}")
        conv_s.append(c_new)

    yp = _norm_drop_lead(xp, final_norm_w, batch, CHUNK, seq, x_prompt.dtype, "norm_out_prompt")
    tm = _pick_tile(n_seq * toks, 512)
    ys = _norm(xs, final_norm_w, x_sample.dtype, tm, (n_seq * toks // tm,), lambda i: (i, 0), n_seq * toks,
               lambda i: (i, 0), "norm_out_sample").reshape(n_seq, toks, d)
    return (yp, ys, jnp.stack(conv_p).astype(x_prompt.dtype), jnp.stack(gdn_p).astype(state_gdn.dtype),
            jnp.stack(hg_p).astype(state_hgrn.dtype), jnp.stack(conv_s).astype(state_gdn_conv.dtype),
            carried[0].astype(state_gdn.dtype), carried[1].astype(state_hgrn.dtype))
```

```python
import functools
import math

import numpy as np
import jax
import jax.numpy as jnp
from jax import lax
from jax.experimental import pallas as pl
from jax.experimental.pallas import tpu as pltpu

HEAD_DIM = 128
CHUNK = 64
CONV_W = 4
EPS = 1e-6
SAMPLE_SEQS_PER_STEP = 8
PROMPT_SEQS_PER_STEP = 2
V7X_VMEM_LIMIT_BYTES = 56 * 1024 * 1024

F32 = jnp.float32
BF16 = jnp.bfloat16


def _mm(a, b):
    return jnp.dot(a.astype(BF16), b.astype(BF16), preferred_element_type=F32)


def _mm_nt(a, b):
    return lax.dot_general(a.astype(BF16), b.astype(BF16), (((1,), (1,)), ((), ())),
                           preferred_element_type=F32)


def _mm_tn(a, b):
    return lax.dot_general(a.astype(BF16), b.astype(BF16), (((0,), (0,)), ((), ())),
                           preferred_element_type=F32)


def _split_bf16(x):
    hi = x.astype(BF16)
    return hi, (x - hi.astype(F32)).astype(BF16)


def _mm_3pass(a, b):
    ah, al = _split_bf16(a)
    bh, bl = _split_bf16(b)
    return (jnp.dot(ah, bh, preferred_element_type=F32) + jnp.dot(ah, bl, preferred_element_type=F32)
            + jnp.dot(al, bh, preferred_element_type=F32))


def _mm_sel(m01, x):
    x1 = x.astype(BF16)
    r1 = x - x1.astype(F32)
    x2 = r1.astype(BF16)
    x3 = (r1 - x2.astype(F32)).astype(BF16)
    return (jnp.dot(m01, x1, preferred_element_type=F32)
            + jnp.dot(m01, x2, preferred_element_type=F32)
            + jnp.dot(m01, x3, preferred_element_type=F32))


def _sigmoid(x):
    return 1.0 / (1.0 + jnp.exp(-x))


def _sigmoid_abs(x):
    return 0.5 * jnp.tanh(0.5 * x) + 0.5


def _silu(x):
    h = 0.5 * x
    return h + h * jnp.tanh(h)


def _softplus(x):
    return jnp.maximum(x, 0.0) + jnp.log(1.0 + jnp.exp(-jnp.abs(x)))


def _pick_tile(m, target, mult=16):
    best = None
    for t in range(mult, m + 1, mult):
        if m % t == 0 and t <= target:
            best = t
    if best is None:
        raise ValueError(f"no tile for {m}")
    return best


def _cparams(sem):
    return pltpu.CompilerParams(dimension_semantics=sem, vmem_limit_bytes=V7X_VMEM_LIMIT_BYTES)


@functools.lru_cache(maxsize=None)
def _block_constants(groups, toks):
    rows = groups * toks
    grp = np.arange(rows) // toks
    pos = np.arange(rows) % toks
    same = grp[:, None] == grp[None, :]
    pt, ps = pos[:, None], pos[None, :]
    prefix = (same & (ps <= pt)).astype(np.float32)
    lev = np.full((rows, rows), -1, np.int32)
    lev[np.arange(rows), np.arange(rows)] = 0
    n, idx = toks, 1
    while n >= 2:
        half = n // 2
        blk = same & ((pt // n) == (ps // n))
        lev[blk & (pt % n >= half) & (ps % n < half)] = idx
        n //= 2
        idx += 1
    return prefix, lev, idx - 1


def _rows_from(b, offset, n):
    rows = b.shape[0]
    if n % 8 == 0:
        return jnp.concatenate(
            [jnp.broadcast_to(b[s + offset:s + offset + 1, :], (n, HEAD_DIM)) for s in range(0, rows, n)], axis=0)
    b3 = b.reshape(rows // 8, 8, HEAD_DIM)
    sub = lax.broadcasted_iota(jnp.int32, b3.shape, 1)
    out = None
    for s in range(0, 8, n):
        cand = jnp.broadcast_to(b3[:, s + offset:s + offset + 1, :], b3.shape)
        out = cand if out is None else jnp.where(sub >= s, cand, out)
    return out.reshape(rows, HEAD_DIM)


def _gdn_blocks(qs, ks, vs, betas, bcols, brows, sufcols, lev, states, toks):
    blocks = range(len(qs))
    rows = qs[0].shape[0]
    groups = range(rows // toks)
    gsl = [slice(g * toks, (g + 1) * toks) for g in groups]
    eye = (lev == 0).astype(F32)
    decay = [jnp.exp(jnp.where(lev >= 0, bcols[p] - brows[p], -jnp.inf)) for p in blocks]
    kb = [ks[p] * betas[p] for p in blocks]
    a = [jnp.where(lev >= 1, _mm_nt(kb[p], ks[p]) * decay[p], 0.0) for p in blocks]
    eb = [jnp.exp(bcols[p]) for p in blocks]
    qe = [qs[p] * eb[p] for p in blocks]
    o_state = [jnp.concatenate([_mm(qe[p][gsl[g]], states[p][g]) for g in groups], axis=0) for p in blocks]
    qk = [_mm_nt(qs[p], ks[p]) * decay[p] for p in blocks]
    inv = [eye - a[p] for p in blocks]
    apow = a
    for _ in range(int(math.log2(toks)) - 1):
        apow = [_mm_3pass(x, x) for x in apow]
        inv = [inv[p] + _mm_3pass(inv[p], apow[p]) for p in blocks]
    sol = [_mm_3pass(inv[p], jnp.concatenate([vs[p] * betas[p], kb[p] * eb[p]], axis=1)) for p in blocks]
    v_new = [sol[p][:, :HEAD_DIM]
             - jnp.concatenate([_mm(sol[p][gsl[g], HEAD_DIM:], states[p][g]) for g in groups], axis=0)
             for p in blocks]
    o = [o_state[p] + _mm(qk[p], v_new[p]) for p in blocks]
    kdec = [ks[p] * jnp.exp(sufcols[p]) for p in blocks]
    new_states = []
    for p in blocks:
        per_group = []
        for g in groups:
            r0 = g * toks
            ebl = jnp.exp(bcols[p][r0:r0 + 1] + sufcols[p][r0:r0 + 1])
            per_group.append(ebl * states[p][g] + _mm_tn(kdec[p][gsl[g]], v_new[p][gsl[g]]))
        new_states.append(per_group)
    return o, new_states


def _hgrn_blocks(qs, f_logits, vs, lbs, prefix, lev, n_levels, states_t, toks):
    blocks = range(len(qs))
    rows = qs[0].shape[0]
    groups = range(rows // toks)
    gsl = [slice(g * toks, (g + 1) * toks) for g in groups]
    fg = [lbs[p] + (1.0 - lbs[p]) * _sigmoid(f_logits[p]) for p in blocks]
    ks = [1.0 - fg[p] for p in blocks]
    b = [_mm_sel(prefix, jnp.log(fg[p])) for p in blocks]
    x_pre = [jnp.exp(b[p]) for p in blocks]
    x_suf = [jnp.exp(-jnp.abs(b[p] - _rows_from(b[p], toks - 1, toks))) for p in blocks]
    a = [jnp.where(lev == 0, _mm_nt(qs[p], ks[p]), 0.0) for p in blocks]
    for i in range(n_levels):
        n = toks >> i
        xl = [jnp.exp(-jnp.abs(b[p] - _rows_from(b[p], n // 2 - 1, n))) for p in blocks]
        a = [a[p] + jnp.where(lev == i + 1, _mm_nt(qs[p] * xl[p], ks[p] * xl[p]), 0.0) for p in blocks]
    o = [jnp.concatenate([_mm_nt((qs[p] * x_pre[p])[gsl[g]], states_t[p][g]) for g in groups], axis=0)
         + _mm(a[p], vs[p]) for p in blocks]
    new_states = []
    for p in blocks:
        kdec = ks[p] * x_suf[p]
        per_group = []
        for g in groups:
            last = g * toks + toks - 1
            per_group.append(states_t[p][g] * x_pre[p][last:last + 1] + _mm_tn(vs[p][gsl[g]], kdec[gsl[g]]))
        new_states.append(per_group)
    return o, new_states


def _head_norm_gate(o, z, w):
    y = o * lax.rsqrt(jnp.mean(o * o, axis=-1, keepdims=True) + EPS)
    return y * w * _silu(z)


def _l2norm(x):
    return x * lax.rsqrt(jnp.sum(x * x, axis=-1, keepdims=True) + EPS)


def _layer_lower_bound(lbp, layer):
    m = jnp.max(lbp, axis=0, keepdims=True)
    e = jnp.exp(lbp - m)
    p = e / jnp.sum(e, axis=0, keepdims=True)
    if layer == 0:
        return jnp.zeros_like(m)
    return jnp.sum(p[1:layer + 1], axis=0, keepdims=True)


def _conv_slab(cbuf, base, rows, cols, w):
    y = (w[3:4] * cbuf[base:base + rows, cols] + w[2:3] * cbuf[base - 1:base - 1 + rows, cols]
         + w[1:2] * cbuf[base - 2:base - 2 + rows, cols] + w[0:1] * cbuf[base - 3:base - 3 + rows, cols])
    return _silu(y)


def _mix_prompt_kernel(pg_ref, ph_ref, ba_ref, convw_ref, alog_ref, dtb_ref, gnw_ref, hnw_ref, lbp_ref,
                       sel_ref, lev_ref, mix_ref, convo_ref, sgo_ref, sho_ref,
                       cbuf, sg, sht, *, heads, layer, n_levels, seqs):
    c = pl.program_id(1)
    kw = heads * HEAD_DIM
    rows = CHUNK

    @pl.when(c == 0)
    def _():
        cbuf[:, 0:8, :] = jnp.zeros((seqs, 8, 3 * kw), F32)
        sg[...] = jnp.zeros_like(sg)
        sht[...] = jnp.zeros_like(sht)

    sel = sel_ref[...]
    lev = lev_ref[...]
    gnw = gnw_ref[...]
    hnw = hnw_ref[...]
    scale = HEAD_DIM ** -0.5
    ps = [(s, h) for s in range(seqs) for h in range(heads)]

    def slab(off, h):
        return slice(off + h * HEAD_DIM, off + (h + 1) * HEAD_DIM)

    def conv(s, off, h):
        return _conv_slab(cbuf.at[s], 8, rows, slab(off, h), convw_ref[:, slab(off, h)])

    for s in range(seqs):
        cbuf[s, 8:8 + rows, :] = pg_ref[s, :, 0:3 * kw]
    qs = [_l2norm(conv(s, 0, h)) * scale for s, h in ps]
    ks = [_l2norm(conv(s, kw, h)) for s, h in ps]
    vs = [conv(s, 2 * kw, h) for s, h in ps]
    for s in range(seqs):
        tail = cbuf[s, 8 + rows - 3:8 + rows, :]
        cbuf[s, 5:8, :] = tail
        convo_ref[s] = tail

    beta_all, b_all, b_all_t, suf_all = [], [], [], []
    for s in range(seqs):
        ba = ba_ref[s]
        beta_all.append(_sigmoid_abs(ba))
        g_all = -jnp.exp(alog_ref[...]) * _softplus(ba + dtb_ref[...])
        b = _mm_sel(sel, g_all)
        b_all.append(b)
        b_all_t.append(b.T)
        suf_all.append(b[rows - 1:rows, :] - b)
    os_, new = _gdn_blocks(qs, ks, vs, [beta_all[s][:, h:h + 1] for s, h in ps],
                           [b_all[s][:, heads + h:heads + h + 1] for s, h in ps],
                           [b_all_t[s][heads + h:heads + h + 1, :] for s, h in ps],
                           [suf_all[s][:, heads + h:heads + h + 1] for s, h in ps], lev,
                           [[sg[s, h]] for s, h in ps], rows)
    for p, (s, h) in enumerate(ps):
        sg[s, h] = new[p][0]
        mix_ref[s, :, slab(0, h)] = _head_norm_gate(os_[p], pg_ref[s, :, slab(3 * kw, h)],
                                                    gnw).astype(mix_ref.dtype)

    lb_all = _layer_lower_bound(lbp_ref[...], layer)
    os_, new = _hgrn_blocks([ph_ref[s, :, slab(0, h)] for s, h in ps],
                            [ph_ref[s, :, slab(kw, h)] for s, h in ps],
                            [ph_ref[s, :, slab(2 * kw, h)] for s, h in ps],
                            [lb_all[:, slab(0, h)] for s, h in ps], sel, lev, n_levels,
                            [[sht[s, h]] for s, h in ps], rows)
    for p, (s, h) in enumerate(ps):
        sht[s, h] = new[p][0]
        mix_ref[s, :, slab(kw, h)] = _head_norm_gate(os_[p], ph_ref[s, :, slab(3 * kw, h)],
                                                     hnw).astype(mix_ref.dtype)

    @pl.when(c == pl.num_programs(1) - 1)
    def _():
        sgo_ref[...] = sg[...]
        for s, h in ps:
            sho_ref[s, h] = sht[s, h].T


def _mix_sample_kernel(pg_ref, ph_ref, ba_ref, convs_ref, sgi_ref, shi_ref, convw_ref, alog_ref, dtb_ref,
                       gnw_ref, hnw_ref, lbp_ref, sel_ref, lev_ref, *rest,
                       heads, layer, n_levels, toks, seqs):
    mix_ref, convo_ref, sgo_ref, sho_ref, cbuf = rest[-5:]
    kw = heads * HEAD_DIM
    sel = sel_ref[...]
    lev = lev_ref[...]
    rows = heads * toks
    gnw = gnw_ref[...]
    hnw = hnw_ref[...]
    scale = HEAD_DIM ** -0.5
    lb_all = _layer_lower_bound(lbp_ref[...], layer)
    lb_rows = jnp.concatenate(
        [jnp.broadcast_to(lb_all[:, h * HEAD_DIM:(h + 1) * HEAD_DIM], (toks, HEAD_DIM)) for h in range(heads)],
        axis=0)

    ss = range(seqs)
    hs = range(heads)
    trs = [slice(s * toks, (s + 1) * toks) for s in ss]

    def slab(off, h):
        return slice(off + h * HEAD_DIM, off + (h + 1) * HEAD_DIM)

    def stack(ref, s, off):
        return jnp.concatenate([ref[trs[s], slab(off, h)] for h in hs], axis=0)

    def conv_stack(s, off):
        return jnp.concatenate(
            [_conv_slab(cbuf.at[s], 8, toks, slab(off, h), convw_ref[:, slab(off, h)]) for h in hs], axis=0)

    for s in ss:
        cbuf[s, 5:8, :] = convs_ref[0, s]
        cbuf[s, 8:8 + toks, :] = pg_ref[trs[s], 0:3 * kw]
    qs = [_l2norm(conv_stack(s, 0)) * scale for s in ss]
    ks = [_l2norm(conv_stack(s, kw)) for s in ss]
    vs = [conv_stack(s, 2 * kw) for s in ss]
    for s in ss:
        convo_ref[s] = cbuf[s, 8 + toks - 3:8 + toks, :]

    betas, bcols, brows, sufcols = [], [], [], []
    for s in ss:
        ba = ba_ref[trs[s], :]
        beta_all = _sigmoid_abs(ba)
        g_all = -jnp.exp(alog_ref[...]) * _softplus(ba + dtb_ref[...])
        betas.append(jnp.concatenate([beta_all[:, h:h + 1] for h in hs], axis=0))
        gcol = jnp.concatenate([g_all[:, heads + h:heads + h + 1] for h in hs], axis=0)
        b_b = _mm_sel(sel, jnp.broadcast_to(gcol, (rows, HEAD_DIM)))
        bcols.append(b_b[:, 0:1])
        brows.append(b_b.T[0:1, :])
        sufcols.append((_rows_from(b_b, toks - 1, toks) - b_b)[:, 0:1])
    os_, new = _gdn_blocks(qs, ks, vs, betas, bcols, brows, sufcols, lev,
                           [[sgi_ref[0, s, h] for h in hs] for s in ss], toks)
    for s in ss:
        og = _head_norm_gate(os_[s], stack(pg_ref, s, 3 * kw), gnw)
        for h in hs:
            sgo_ref[0, s, h] = new[s][h]
            mix_ref[trs[s], slab(0, h)] = og[h * toks:(h + 1) * toks].astype(mix_ref.dtype)

    os_, new = _hgrn_blocks([stack(ph_ref, s, 0) for s in ss], [stack(ph_ref, s, kw) for s in ss],
                            [stack(ph_ref, s, 2 * kw) for s in ss], [lb_rows for s in ss], sel, lev, n_levels,
                            [[shi_ref[0, s, h].T for h in hs] for s in ss], toks)
    for s in ss:
        oh = _head_norm_gate(os_[s], stack(ph_ref, s, 3 * kw), hnw)
        for h in hs:
            sho_ref[0, s, h] = new[s][h].T
            mix_ref[trs[s], slab(kw, h)] = oh[h * toks:(h + 1) * toks].astype(mix_ref.dtype)


def _mixer_params(layer, heads, conv_w, a_log, dt_bias, gdn_norm_w, hgrn_norm_w, hgrn_lb):
    pad = HEAD_DIM - 2 * heads
    alog = jnp.pad(a_log[layer][None, :], ((0, 0), (heads, pad)))
    dtb = jnp.pad(dt_bias[layer][None, :], ((0, 0), (heads, pad)))
    return (conv_w[layer], alog, dtb, gdn_norm_w[layer][None, :], hgrn_norm_w[layer][None, :], hgrn_lb)


def _full_spec(a):
    nd = a.ndim
    return pl.BlockSpec(a.shape, lambda *_: (0,) * nd)


def _mix_prompt(projs, params, layer, batch, heads):
    n_rows = projs[0].shape[0]
    seq_rows = n_rows // batch
    seqs = PROMPT_SEQS_PER_STEP if batch % PROMPT_SEQS_PER_STEP == 0 else 1
    kw = heads * HEAD_DIM
    sel_np, lev_np, n_levels = _block_constants(1, CHUNK)
    sel = jnp.asarray(sel_np, BF16)
    lev = jnp.asarray(lev_np)
    row_map = lambda b, c: (b, c, 0)
    consts = params + (sel, lev)
    projs = [p.reshape(batch, seq_rows, p.shape[1]) for p in projs]
    kern = functools.partial(_mix_prompt_kernel, heads=heads, layer=layer, n_levels=n_levels, seqs=seqs)
    mix, conv_new, gdn_new, hg_new = pl.pallas_call(
        kern,
        grid=(batch // seqs, seq_rows // CHUNK),
        in_specs=[pl.BlockSpec((seqs, CHUNK, p.shape[2]), row_map) for p in projs]
                 + [_full_spec(a) for a in consts],
        out_specs=[pl.BlockSpec((seqs, CHUNK, 2 * kw), row_map),
                   pl.BlockSpec((seqs, CONV_W - 1, 3 * kw), lambda b, c: (b, 0, 0)),
                   pl.BlockSpec((seqs, heads, HEAD_DIM, HEAD_DIM), lambda b, c: (b, 0, 0, 0)),
                   pl.BlockSpec((seqs, heads, HEAD_DIM, HEAD_DIM), lambda b, c: (b, 0, 0, 0))],
        out_shape=[jax.ShapeDtypeStruct((batch, seq_rows, 2 * kw), BF16),
                   jax.ShapeDtypeStruct((batch, CONV_W - 1, 3 * kw), F32),
                   jax.ShapeDtypeStruct((batch, heads, HEAD_DIM, HEAD_DIM), F32),
                   jax.ShapeDtypeStruct((batch, heads, HEAD_DIM, HEAD_DIM), F32)],
        scratch_shapes=[pltpu.VMEM((seqs, 8 + CHUNK, 3 * kw), F32),
                        pltpu.VMEM((seqs, heads, HEAD_DIM, HEAD_DIM), F32),
                        pltpu.VMEM((seqs, heads, HEAD_DIM, HEAD_DIM), F32)],
        compiler_params=_cparams(("arbitrary", "arbitrary")),
        name=f"mix_prompt_l{layer}",
    )(*projs, *consts)
    return mix.reshape(n_rows, 2 * kw), conv_new, gdn_new, hg_new


def _mix_sample(projs, conv_state, gdn_state, hgrn_state, params, layer, n_seq, toks, heads, carried):
    kw = heads * HEAD_DIM
    seqs = math.gcd(n_seq, SAMPLE_SEQS_PER_STEP)
    assert (seqs * toks) % 16 == 0
    sel_np, lev_np, n_levels = _block_constants(heads, toks)
    sel = jnp.asarray(sel_np, BF16)
    lev = jnp.asarray(lev_np)
    consts = params + (sel, lev)
    row_map = lambda n: (n, 0)
    kern = functools.partial(_mix_sample_kernel, heads=heads, layer=layer, n_levels=n_levels, toks=toks,
                             seqs=seqs)
    state_blk = pl.BlockSpec((1, seqs, heads, HEAD_DIM, HEAD_DIM), lambda n: (layer, n, 0, 0, 0))
    in_specs = [pl.BlockSpec((seqs * toks, p.shape[1]), row_map) for p in projs] + [
        pl.BlockSpec((1, seqs, CONV_W - 1, 3 * kw), lambda n: (layer, n, 0, 0)),
        state_blk, state_blk] + [_full_spec(a) for a in consts]
    args = tuple(projs) + (conv_state, gdn_state, hgrn_state) + consts
    aliases = {len(args): 2, len(args) + 1: 3}
    in_specs += [pl.BlockSpec(memory_space=pl.ANY), pl.BlockSpec(memory_space=pl.ANY)]
    args += tuple(carried)
    return pl.pallas_call(
        kern,
        grid=(n_seq // seqs,),
        in_specs=in_specs,
        out_specs=[pl.BlockSpec((seqs * toks, 2 * kw), row_map),
                   pl.BlockSpec((seqs, CONV_W - 1, 3 * kw), lambda n: (n, 0, 0)),
                   state_blk, state_blk],
        out_shape=[jax.ShapeDtypeStruct((n_seq * toks, 2 * kw), BF16),
                   jax.ShapeDtypeStruct((n_seq, CONV_W - 1, 3 * kw), F32),
                   jax.ShapeDtypeStruct(gdn_state.shape, F32),
                   jax.ShapeDtypeStruct(hgrn_state.shape, F32)],
        scratch_shapes=[pltpu.VMEM((seqs, 8 + toks, 3 * kw), F32)],
        input_output_aliases=aliases,
        compiler_params=_cparams(("arbitrary",)),
        name=f"mix_sample_l{layer}",
    )(*args)


def _rmsnorm_rows(x, w):
    return x * lax.rsqrt(jnp.mean(x * x, axis=-1, keepdims=True) + EPS) * w


def _norm_kernel(x_ref, w_ref, o_ref):
    o_ref[...] = _rmsnorm_rows(x_ref[...], w_ref[...]).astype(o_ref.dtype)


def _norm(x, w, out_dtype, block_rows, grid, in_map, out_rows, out_map, name):
    d = x.shape[1]
    return pl.pallas_call(
        _norm_kernel,
        grid=grid,
        in_specs=[pl.BlockSpec((block_rows, d), in_map), pl.BlockSpec((1, d), lambda *_: (0, 0))],
        out_specs=pl.BlockSpec((block_rows, d), out_map),
        out_shape=jax.ShapeDtypeStruct((out_rows, d), out_dtype),
        compiler_params=_cparams(("arbitrary",) * len(grid)),
        name=name,
    )(x, w[None, :])


def _norm_drop_lead(x, w, batch, lead, seq, out_dtype, name):
    d = x.shape[1]
    tr = _pick_tile(seq, 512)
    return pl.pallas_call(
        _norm_kernel,
        grid=(batch, seq // tr),
        in_specs=[pl.BlockSpec((pl.Element(1), pl.Element(tr), pl.Element(d)),
                               lambda b, r: (b, pl.multiple_of(lead + r * tr, 8), 0)),
                  pl.BlockSpec((1, d), lambda b, r: (0, 0))],
        out_specs=pl.BlockSpec((1, tr, d), lambda b, r: (b, r, 0)),
        out_shape=jax.ShapeDtypeStruct((batch, seq, d), out_dtype),
        compiler_params=_cparams(("arbitrary", "arbitrary")),
        name=name,
    )(x.reshape(batch, lead + seq, d), w[None, :])


def _proj_kernel(h_ref, wt_ref, o_ref, *zero_refs):
    o_ref[...] = lax.dot_general(h_ref[...], wt_ref[0].astype(BF16), (((1,), (1,)), ((), ())),
                                 preferred_element_type=F32)
    for z_ref in zero_refs:
        z_ref[...] = jnp.zeros_like(z_ref)


def _proj(h, w_t, layer, row0, n, name, zero_fill=None):
    m, d = h.shape
    tm = _pick_tile(m, 2112)
    tn = _pick_tile(n, 512, HEAD_DIM)
    assert row0 % 8 == 0
    grid = (m // tm, n // tn)
    out_specs = [pl.BlockSpec((tm, tn), lambda i, j: (i, j))]
    out_shape = [jax.ShapeDtypeStruct((m, n), F32)]
    if zero_fill is not None:
        slab_rows = math.prod(zero_fill) // HEAD_DIM // (grid[0] * grid[1])
        if slab_rows % 8 or slab_rows * HEAD_DIM * grid[0] * grid[1] != math.prod(zero_fill):
            return _proj(h, w_t, layer, row0, n, name), jnp.zeros(zero_fill, F32)
        out_specs.append(pl.BlockSpec((slab_rows, HEAD_DIM), lambda i, j: (i * grid[1] + j, 0)))
        out_shape.append(jax.ShapeDtypeStruct((slab_rows * grid[0] * grid[1], HEAD_DIM), F32))
    outs = pl.pallas_call(
        _proj_kernel,
        grid=grid,
        in_specs=[pl.BlockSpec((tm, d), lambda i, j: (i, 0)),
                  pl.BlockSpec((pl.Element(1), pl.Element(tn), pl.Element(d)),
                               lambda i, j: (layer, pl.multiple_of(row0 + j * tn, 8), 0))],
        out_specs=out_specs,
        out_shape=out_shape,
        compiler_params=_cparams(("arbitrary", "arbitrary")),
        name=name,
    )(h, w_t)
    if zero_fill is None:
        return outs[0]
    return outs[0], outs[1].reshape(zero_fill)


def _wout_kernel(x_ref, mix_ref, w_ref, nw_ref, x1_ref, h_ref):
    x1 = x_ref[...] + jnp.dot(mix_ref[...], w_ref[...], preferred_element_type=F32)
    x1_ref[...] = x1
    h_ref[...] = _rmsnorm_rows(x1, nw_ref[...]).astype(h_ref.dtype)


def _wout(x, mix, w_out, layer, norm_w, name):
    m, d = x.shape
    tm = _pick_tile(m, 352)
    return pl.pallas_call(
        _wout_kernel,
        grid=(m // tm,),
        in_specs=[pl.BlockSpec((tm, d), lambda i: (i, 0)),
                  pl.BlockSpec((tm, mix.shape[1]), lambda i: (i, 0)),
                  pl.BlockSpec((None,) + w_out.shape[1:], lambda i: (layer, 0, 0)),
                  pl.BlockSpec((1, d), lambda i: (0, 0))],
        out_specs=[pl.BlockSpec((tm, d), lambda i: (i, 0)), pl.BlockSpec((tm, d), lambda i: (i, 0))],
        out_shape=[jax.ShapeDtypeStruct((m, d), F32), jax.ShapeDtypeStruct((m, d), BF16)],
        compiler_params=_cparams(("arbitrary",)),
        name=name,
    )(x, mix, w_out, norm_w[None, :])


def _mlp_kernel(x_ref, h_ref, wup_ref, wdn_ref, nw_ref, xo_ref, ho_ref):
    f = pl.program_id(1)

    @pl.when(f == 0)
    def _():
        xo_ref[...] = x_ref[...]

    u = jnp.maximum(jnp.dot(h_ref[...], wup_ref[...], preferred_element_type=F32), 0.0)
    xo_ref[...] += jnp.dot((u * u).astype(BF16), wdn_ref[...], preferred_element_type=F32)

    @pl.when(f == pl.num_programs(1) - 1)
    def _():
        ho_ref[...] = _rmsnorm_rows(xo_ref[...], nw_ref[...]).astype(ho_ref.dtype)


def _mlp(x, h, w_up, w_down, layer, next_norm_w, name):
    m, d = x.shape
    ff = w_up.shape[2]
    tm = _pick_tile(m, 704)
    tf = _pick_tile(ff, 512, HEAD_DIM)
    return pl.pallas_call(
        _mlp_kernel,
        grid=(m // tm, ff // tf),
        in_specs=[pl.BlockSpec((tm, d), lambda i, f: (i, 0)),
                  pl.BlockSpec((tm, d), lambda i, f: (i, 0)),
                  pl.BlockSpec((None, d, tf), lambda i, f: (layer, 0, f)),
                  pl.BlockSpec((None, tf, d), lambda i, f: (layer, f, 0)),
                  pl.BlockSpec((1, d), lambda i, f: (0, 0))],
        out_specs=[pl.BlockSpec((tm, d), lambda i, f: (i, 0)), pl.BlockSpec((tm, d), lambda i, f: (i, 0))],
        out_shape=[jax.ShapeDtypeStruct((m, d), F32), jax.ShapeDtypeStruct((m, d), BF16)],
        compiler_params=_cparams(("arbitrary", "arbitrary")),
        name=name,
    )(x, h, w_up, w_down, next_norm_w[None, :])


def kernel(x_prompt, x_sample, state_gdn_conv, state_gdn, state_hgrn, meta_tokens, norm1_w, w_in, conv_w,
           a_log, dt_bias, gdn_norm_w, hgrn_lb, hgrn_norm_w, w_out, norm2_w, w_up, w_down, final_norm_w):
    batch, seq, d = x_prompt.shape
    n_seq, toks, _ = x_sample.shape
    depth = w_in.shape[0]
    heads = a_log.shape[1]
    kw = heads * HEAD_DIM
    n_meta = meta_tokens.shape[0]
    assert seq % CHUNK == 0 and 0 < n_meta <= CHUNK
    assert w_in.shape[2] == 8 * kw + 2 * heads and hgrn_lb.shape[1] == kw and 2 * heads <= HEAD_DIM

    w_in_t = jnp.swapaxes(w_in, 1, 2)
    w_ba_t = jnp.pad(w_in_t[:, 4 * kw:4 * kw + 2 * heads, :],
                     ((0, 0), (0, HEAD_DIM - 2 * heads), (0, 0))).astype(BF16)
    w_out_b = w_out.astype(BF16)
    w_up_b = w_up.astype(BF16)
    w_down_b = w_down.astype(BF16)

    lead = CHUNK - n_meta
    seq_rows = CHUNK + seq
    xp = jnp.concatenate([jnp.zeros((batch, lead, d), x_prompt.dtype),
                          jnp.broadcast_to(meta_tokens.astype(x_prompt.dtype)[None], (batch, n_meta, d)),
                          x_prompt], axis=1).reshape(batch * seq_rows, d)
    xs = x_sample.reshape(n_seq * toks, d)

    def first_norm(x, name):
        tm = _pick_tile(x.shape[0], 512)
        return _norm(x, norm1_w[0], BF16, tm, (x.shape[0] // tm,), lambda i: (i, 0), x.shape[0],
                     lambda i: (i, 0), name)

    hp = first_norm(xp, "norm_in_prompt")
    hs = first_norm(xs, "norm_in_sample")
    conv_p, gdn_p, hg_p, conv_s = [], [], [], []
    carried = None
    for l in range(depth):
        params = _mixer_params(l, heads, conv_w, a_log, dt_bias, gdn_norm_w, hgrn_norm_w, hgrn_lb)
        next_w = norm1_w[l + 1] if l + 1 < depth else final_norm_w
        pg = _proj(hp, w_in_t, l, 0, 4 * kw, f"proj_gdn_prompt_l{l}",
                   zero_fill=state_gdn.shape if l == 0 else None)
        ph = _proj(hp, w_in_t, l, 4 * kw + 2 * heads, 4 * kw, f"proj_hgrn_prompt_l{l}",
                   zero_fill=state_hgrn.shape if l == 0 else None)
        if l == 0:
            (pg, acc_g), (ph, acc_h) = pg, ph
            carried = (acc_g, acc_h)
        projs = (pg, ph, _proj(hp, w_ba_t, l, 0, HEAD_DIM, f"proj_ba_prompt_l{l}"))
        mix, c_new, g_new, h_new = _mix_prompt(projs, params, l, batch, heads)
        x1, h2 = _wout(xp, mix, w_out_b, l, norm2_w[l], f"wout_prompt_l{l}")
        xp, hp = _mlp(x1, h2, w_up_b, w_down_b, l, next_w, f"mlp_prompt_l{l}")
        conv_p.append(c_new); gdn_p.append(g_new); hg_p.append(h_new)
        projs = (_proj(hs, w_in_t, l, 0, 4 * kw, f"proj_gdn_sample_l{l}"),
                 _proj(hs, w_in_t, l, 4 * kw + 2 * heads, 4 * kw, f"proj_hgrn_sample_l{l}"),
                 _proj(hs, w_ba_t, l, 0, HEAD_DIM, f"proj_ba_sample_l{l}"))
        mix, c_new, g_new, h_new = _mix_sample(projs, state_gdn_conv, state_gdn, state_hgrn, params, l,
                                               n_seq, toks, heads, carried)
        carried = (g_new, h_new)
        x1, h2 = _wout(xs, mix, w_out_b, l, norm2_w[l], f"wout_sample_l{l}")
        xs, hs = _mlp(x1, h2, w_up_b, w_down_b, l, next_w, f"mlp_sample_l{l}")
        conv_s.append(c_new)

    yp = _norm_drop_lead(xp, final_norm_w, batch, CHUNK, seq, x_prompt.dtype, "norm_out_prompt")
    tm = _pick_tile(n_seq * toks, 512)
    ys = _norm(xs, final_norm_w, x_sample.dtype, tm, (n_seq * toks // tm,), lambda i: (i, 0), n_seq * toks,
               lambda i: (i, 0), "norm_out_sample").reshape(n_seq, toks, d)
    return (yp, ys, jnp.stack(conv_p).astype(x_prompt.dtype), jnp.stack(gdn_p).astype(state_gdn.dtype),
            jnp.stack(hg_p).astype(state_hgrn.dtype), jnp.stack(conv_s).astype(state_gdn_conv.dtype),
            carried[0].astype(state_gdn.dtype), carried[1].astype(state_hgrn.dtype))
```

```python
import functools
import math

import numpy as np
import jax
import jax.numpy as jnp
from jax import lax
from jax.experimental import pallas as pl
from jax.experimental.pallas import tpu as pltpu

HEAD_DIM = 128
CHUNK = 64
CONV_W = 4
EPS = 1e-6
SAMPLE_SEQS_PER_STEP = 8
PROMPT_SEQS_PER_STEP = 4
V7X_VMEM_LIMIT_BYTES = 56 * 1024 * 1024

F32 = jnp.float32
BF16 = jnp.bfloat16


def _mm(a, b):
    return jnp.dot(a.astype(BF16), b.astype(BF16), preferred_element_type=F32)


def _mm_nt(a, b):
    return lax.dot_general(a.astype(BF16), b.astype(BF16), (((1,), (1,)), ((), ())),
                           preferred_element_type=F32)


def _mm_tn(a, b):
    return lax.dot_general(a.astype(BF16), b.astype(BF16), (((0,), (0,)), ((), ())),
                           preferred_element_type=F32)


def _split_bf16(x):
    hi = x.astype(BF16)
    return hi, (x - hi.astype(F32)).astype(BF16)


def _mm_3pass(a, b):
    ah, al = _split_bf16(a)
    bh, bl = _split_bf16(b)
    return (jnp.dot(ah, bh, preferred_element_type=F32) + jnp.dot(ah, bl, preferred_element_type=F32)
            + jnp.dot(al, bh, preferred_element_type=F32))


def _mm_sel(m01, x):
    x1 = x.astype(BF16)
    r1 = x - x1.astype(F32)
    x2 = r1.astype(BF16)
    x3 = (r1 - x2.astype(F32)).astype(BF16)
    return (jnp.dot(m01, x1, preferred_element_type=F32)
            + jnp.dot(m01, x2, preferred_element_type=F32)
            + jnp.dot(m01, x3, preferred_element_type=F32))


def _sigmoid(x):
    return 1.0 / (1.0 + jnp.exp(-x))


def _sigmoid_abs(x):
    return 0.5 * jnp.tanh(0.5 * x) + 0.5


def _silu(x):
    h = 0.5 * x
    return h + h * jnp.tanh(h)


def _softplus(x):
    return jnp.maximum(x, 0.0) + jnp.log(1.0 + jnp.exp(-jnp.abs(x)))


def _pick_tile(m, target, mult=16):
    best = None
    for t in range(mult, m + 1, mult):
        if m % t == 0 and t <= target:
            best = t
    if best is None:
        raise ValueError(f"no tile for {m}")
    return best


def _cparams(sem):
    return pltpu.CompilerParams(dimension_semantics=sem, vmem_limit_bytes=V7X_VMEM_LIMIT_BYTES)


@functools.lru_cache(maxsize=None)
def _block_constants(groups, toks):
    rows = groups * toks
    grp = np.arange(rows) // toks
    pos = np.arange(rows) % toks
    same = grp[:, None] == grp[None, :]
    pt, ps = pos[:, None], pos[None, :]
    prefix = (same & (ps <= pt)).astype(np.float32)
    lev = np.full((rows, rows), -1, np.int32)
    lev[np.arange(rows), np.arange(rows)] = 0
    n, idx = toks, 1
    while n >= 2:
        half = n // 2
        blk = same & ((pt // n) == (ps // n))
        lev[blk & (pt % n >= half) & (ps % n < half)] = idx
        n //= 2
        idx += 1
    return prefix, lev, idx - 1


def _rows_from(b, offset, n):
    rows = b.shape[0]
    if n % 8 == 0:
        return jnp.concatenate(
            [jnp.broadcast_to(b[s + offset:s + offset + 1, :], (n, HEAD_DIM)) for s in range(0, rows, n)], axis=0)
    b3 = b.reshape(rows // 8, 8, HEAD_DIM)
    sub = lax.broadcasted_iota(jnp.int32, b3.shape, 1)
    out = None
    for s in range(0, 8, n):
        cand = jnp.broadcast_to(b3[:, s + offset:s + offset + 1, :], b3.shape)
        out = cand if out is None else jnp.where(sub >= s, cand, out)
    return out.reshape(rows, HEAD_DIM)


def _gdn_blocks(qs, ks, vs, betas, bcols, brows, sufcols, lev, states, toks):
    blocks = range(len(qs))
    rows = qs[0].shape[0]
    groups = range(rows // toks)
    gsl = [slice(g * toks, (g + 1) * toks) for g in groups]
    eye = (lev == 0).astype(F32)
    decay = [jnp.exp(jnp.where(lev >= 0, bcols[p] - brows[p], -jnp.inf)) for p in blocks]
    kb = [ks[p] * betas[p] for p in blocks]
    a = [jnp.where(lev >= 1, _mm_nt(kb[p], ks[p]) * decay[p], 0.0) for p in blocks]
    eb = [jnp.exp(bcols[p]) for p in blocks]
    qe = [qs[p] * eb[p] for p in blocks]
    o_state = [jnp.concatenate([_mm(qe[p][gsl[g]], states[p][g]) for g in groups], axis=0) for p in blocks]
    qk = [_mm_nt(qs[p], ks[p]) * decay[p] for p in blocks]
    inv = [eye - a[p] for p in blocks]
    apow = a
    for _ in range(int(math.log2(toks)) - 1):
        apow = [_mm_3pass(x, x) for x in apow]
        inv = [inv[p] + _mm_3pass(inv[p], apow[p]) for p in blocks]
    sol = [_mm_3pass(inv[p], jnp.concatenate([vs[p] * betas[p], kb[p] * eb[p]], axis=1)) for p in blocks]
    v_new = [sol[p][:, :HEAD_DIM]
             - jnp.concatenate([_mm(sol[p][gsl[g], HEAD_DIM:], states[p][g]) for g in groups], axis=0)
             for p in blocks]
    o = [o_state[p] + _mm(qk[p], v_new[p]) for p in blocks]
    kdec = [ks[p] * jnp.exp(sufcols[p]) for p in blocks]
    new_states = []
    for p in blocks:
        per_group = []
        for g in groups:
            r0 = g * toks
            ebl = jnp.exp(bcols[p][r0:r0 + 1] + sufcols[p][r0:r0 + 1])
            per_group.append(ebl * states[p][g] + _mm_tn(kdec[p][gsl[g]], v_new[p][gsl[g]]))
        new_states.append(per_group)
    return o, new_states


def _hgrn_blocks(qs, f_logits, vs, lbs, prefix, lev, n_levels, states_t, toks):
    blocks = range(len(qs))
    rows = qs[0].shape[0]
    groups = range(rows // toks)
    gsl = [slice(g * toks, (g + 1) * toks) for g in groups]
    fg = [lbs[p] + (1.0 - lbs[p]) * _sigmoid(f_logits[p]) for p in blocks]
    ks = [1.0 - fg[p] for p in blocks]
    b = [_mm_sel(prefix, jnp.log(fg[p])) for p in blocks]
    x_pre = [jnp.exp(b[p]) for p in blocks]
    x_suf = [jnp.exp(-jnp.abs(b[p] - _rows_from(b[p], toks - 1, toks))) for p in blocks]
    a = [jnp.where(lev == 0, _mm_nt(qs[p], ks[p]), 0.0) for p in blocks]
    for i in range(n_levels):
        n = toks >> i
        xl = [jnp.exp(-jnp.abs(b[p] - _rows_from(b[p], n // 2 - 1, n))) for p in blocks]
        a = [a[p] + jnp.where(lev == i + 1, _mm_nt(qs[p] * xl[p], ks[p] * xl[p]), 0.0) for p in blocks]
    o = [jnp.concatenate([_mm_nt((qs[p] * x_pre[p])[gsl[g]], states_t[p][g]) for g in groups], axis=0)
         + _mm(a[p], vs[p]) for p in blocks]
    new_states = []
    for p in blocks:
        kdec = ks[p] * x_suf[p]
        per_group = []
        for g in groups:
            last = g * toks + toks - 1
            per_group.append(states_t[p][g] * x_pre[p][last:last + 1] + _mm_tn(vs[p][gsl[g]], kdec[gsl[g]]))
        new_states.append(per_group)
    return o, new_states


def _head_norm_gate(o, z, w):
    y = o * lax.rsqrt(jnp.mean(o * o, axis=-1, keepdims=True) + EPS)
    return y * w * _silu(z)


def _l2norm(x):
    return x * lax.rsqrt(jnp.sum(x * x, axis=-1, keepdims=True) + EPS)


def _layer_lower_bound(lbp, layer):
    m = jnp.max(lbp, axis=0, keepdims=True)
    e = jnp.exp(lbp - m)
    p = e / jnp.sum(e, axis=0, keepdims=True)
    if layer == 0:
        return jnp.zeros_like(m)
    return jnp.sum(p[1:layer + 1], axis=0, keepdims=True)


def _conv_slab(cbuf, base, rows, cols, w):
    y = (w[3:4] * cbuf[base:base + rows, cols] + w[2:3] * cbuf[base - 1:base - 1 + rows, cols]
         + w[1:2] * cbuf[base - 2:base - 2 + rows, cols] + w[0:1] * cbuf[base - 3:base - 3 + rows, cols])
    return _silu(y)


def _mix_prompt_kernel(pg_ref, ph_ref, ba_ref, convw_ref, alog_ref, dtb_ref, gnw_ref, hnw_ref, lbp_ref,
                       sel_ref, lev_ref, mix_ref, convo_ref, sgo_ref, sho_ref,
                       cbuf, sg, sht, *, heads, layer, n_levels, seqs):
    c = pl.program_id(1)
    kw = heads * HEAD_DIM
    rows = CHUNK

    @pl.when(c == 0)
    def _():
        cbuf[:, 0:8, :] = jnp.zeros((seqs, 8, 3 * kw), F32)
        sg[...] = jnp.zeros_like(sg)
        sht[...] = jnp.zeros_like(sht)

    sel = sel_ref[...]
    lev = lev_ref[...]
    gnw = gnw_ref[...]
    hnw = hnw_ref[...]
    scale = HEAD_DIM ** -0.5
    ps = [(s, h) for s in range(seqs) for h in range(heads)]

    def slab(off, h):
        return slice(off + h * HEAD_DIM, off + (h + 1) * HEAD_DIM)

    def conv(s, off, h):
        return _conv_slab(cbuf.at[s], 8, rows, slab(off, h), convw_ref[:, slab(off, h)])

    for s in range(seqs):
        cbuf[s, 8:8 + rows, :] = pg_ref[s, :, 0:3 * kw]
    qs = [_l2norm(conv(s, 0, h)) * scale for s, h in ps]
    ks = [_l2norm(conv(s, kw, h)) for s, h in ps]
    vs = [conv(s, 2 * kw, h) for s, h in ps]
    for s in range(seqs):
        tail = cbuf[s, 8 + rows - 3:8 + rows, :]
        cbuf[s, 5:8, :] = tail
        convo_ref[s] = tail

    beta_all, b_all, b_all_t, suf_all = [], [], [], []
    for s in range(seqs):
        ba = ba_ref[s]
        beta_all.append(_sigmoid_abs(ba))
        g_all = -jnp.exp(alog_ref[...]) * _softplus(ba + dtb_ref[...])
        b = _mm_sel(sel, g_all)
        b_all.append(b)
        b_all_t.append(b.T)
        suf_all.append(b[rows - 1:rows, :] - b)
    os_, new = _gdn_blocks(qs, ks, vs, [beta_all[s][:, h:h + 1] for s, h in ps],
                           [b_all[s][:, heads + h:heads + h + 1] for s, h in ps],
                           [b_all_t[s][heads + h:heads + h + 1, :] for s, h in ps],
                           [suf_all[s][:, heads + h:heads + h + 1] for s, h in ps], lev,
                           [[sg[s, h]] for s, h in ps], rows)
    for p, (s, h) in enumerate(ps):
        sg[s, h] = new[p][0]
        mix_ref[s, :, slab(0, h)] = _head_norm_gate(os_[p], pg_ref[s, :, slab(3 * kw, h)],
                                                    gnw).astype(mix_ref.dtype)

    lb_all = _layer_lower_bound(lbp_ref[...], layer)
    os_, new = _hgrn_blocks([ph_ref[s, :, slab(0, h)] for s, h in ps],
                            [ph_ref[s, :, slab(kw, h)] for s, h in ps],
                            [ph_ref[s, :, slab(2 * kw, h)] for s, h in ps],
                            [lb_all[:, slab(0, h)] for s, h in ps], sel, lev, n_levels,
                            [[sht[s, h]] for s, h in ps], rows)
    for p, (s, h) in enumerate(ps):
        sht[s, h] = new[p][0]
        mix_ref[s, :, slab(kw, h)] = _head_norm_gate(os_[p], ph_ref[s, :, slab(3 * kw, h)],
                                                     hnw).astype(mix_ref.dtype)

    @pl.when(c == pl.num_programs(1) - 1)
    def _():
        sgo_ref[...] = sg[...]
        for s, h in ps:
            sho_ref[s, h] = sht[s, h].T


def _mix_sample_kernel(pg_ref, ph_ref, ba_ref, convs_ref, sgi_ref, shi_ref, convw_ref, alog_ref, dtb_ref,
                       gnw_ref, hnw_ref, lbp_ref, sel_ref, lev_ref, *rest,
                       heads, layer, n_levels, toks, seqs):
    mix_ref, convo_ref, sgo_ref, sho_ref, cbuf = rest[-5:]
    kw = heads * HEAD_DIM
    sel = sel_ref[...]
    lev = lev_ref[...]
    rows = heads * toks
    gnw = gnw_ref[...]
    hnw = hnw_ref[...]
    scale = HEAD_DIM ** -0.5
    lb_all = _layer_lower_bound(lbp_ref[...], layer)
    lb_rows = jnp.concatenate(
        [jnp.broadcast_to(lb_all[:, h * HEAD_DIM:(h + 1) * HEAD_DIM], (toks, HEAD_DIM)) for h in range(heads)],
        axis=0)

    ss = range(seqs)
    hs = range(heads)
    trs = [slice(s * toks, (s + 1) * toks) for s in ss]

    def slab(off, h):
        return slice(off + h * HEAD_DIM, off + (h + 1) * HEAD_DIM)

    def stack(ref, s, off):
        return jnp.concatenate([ref[trs[s], slab(off, h)] for h in hs], axis=0)

    def conv_stack(s, off):
        return jnp.concatenate(
            [_conv_slab(cbuf.at[s], 8, toks, slab(off, h), convw_ref[:, slab(off, h)]) for h in hs], axis=0)

    for s in ss:
        cbuf[s, 5:8, :] = convs_ref[0, s]
        cbuf[s, 8:8 + toks, :] = pg_ref[trs[s], 0:3 * kw]
    qs = [_l2norm(conv_stack(s, 0)) * scale for s in ss]
    ks = [_l2norm(conv_stack(s, kw)) for s in ss]
    vs = [conv_stack(s, 2 * kw) for s in ss]
    for s in ss:
        convo_ref[s] = cbuf[s, 8 + toks - 3:8 + toks, :]

    betas, bcols, brows, sufcols = [], [], [], []
    for s in ss:
        ba = ba_ref[trs[s], :]
        beta_all = _sigmoid_abs(ba)
        g_all = -jnp.exp(alog_ref[...]) * _softplus(ba + dtb_ref[...])
        betas.append(jnp.concatenate([beta_all[:, h:h + 1] for h in hs], axis=0))
        gcol = jnp.concatenate([g_all[:, heads + h:heads + h + 1] for h in hs], axis=0)
        b_b = _mm_sel(sel, jnp.broadcast_to(gcol, (rows, HEAD_DIM)))
        bcols.append(b_b[:, 0:1])
        brows.append(b_b.T[0:1, :])
        sufcols.append((_rows_from(b_b, toks - 1, toks) - b_b)[:, 0:1])
    os_, new = _gdn_blocks(qs, ks, vs, betas, bcols, brows, sufcols, lev,
                           [[sgi_ref[0, s, h] for h in hs] for s in ss], toks)
    for s in ss:
        og = _head_norm_gate(os_[s], stack(pg_ref, s, 3 * kw), gnw)
        for h in hs:
            sgo_ref[0, s, h] = new[s][h]
            mix_ref[trs[s], slab(0, h)] = og[h * toks:(h + 1) * toks].astype(mix_ref.dtype)

    os_, new = _hgrn_blocks([stack(ph_ref, s, 0) for s in ss], [stack(ph_ref, s, kw) for s in ss],
                            [stack(ph_ref, s, 2 * kw) for s in ss], [lb_rows for s in ss], sel, lev, n_levels,
                            [[shi_ref[0, s, h].T for h in hs] for s in ss], toks)
    for s in ss:
        oh = _head_norm_gate(os_[s], stack(ph_ref, s, 3 * kw), hnw)
        for h in hs:
            sho_ref[0, s, h] = new[s][h].T
            mix_ref[trs[s], slab(kw, h)] = oh[h * toks:(h + 1) * toks].astype(mix_ref.dtype)


def _mixer_params(layer, heads, conv_w, a_log, dt_bias, gdn_norm_w, hgrn_norm_w, hgrn_lb):
    pad = HEAD_DIM - 2 * heads
    alog = jnp.pad(a_log[layer][None, :], ((0, 0), (heads, pad)))
    dtb = jnp.pad(dt_bias[layer][None, :], ((0, 0), (heads, pad)))
    return (conv_w[layer], alog, dtb, gdn_norm_w[layer][None, :], hgrn_norm_w[layer][None, :], hgrn_lb)


def _full_spec(a):
    nd = a.ndim
    return pl.BlockSpec(a.shape, lambda *_: (0,) * nd)


def _mix_prompt(projs, params, layer, batch, heads):
    n_rows = projs[0].shape[0]
    seq_rows = n_rows // batch
    seqs = PROMPT_SEQS_PER_STEP if batch % PROMPT_SEQS_PER_STEP == 0 else 1
    kw = heads * HEAD_DIM
    sel_np, lev_np, n_levels = _block_constants(1, CHUNK)
    sel = jnp.asarray(sel_np, BF16)
    lev = jnp.asarray(lev_np)
    row_map = lambda b, c: (b, c, 0)
    consts = params + (sel, lev)
    projs = [p.reshape(batch, seq_rows, p.shape[1]) for p in projs]
    kern = functools.partial(_mix_prompt_kernel, heads=heads, layer=layer, n_levels=n_levels, seqs=seqs)
    mix, conv_new, gdn_new, hg_new = pl.pallas_call(
        kern,
        grid=(batch // seqs, seq_rows // CHUNK),
        in_specs=[pl.BlockSpec((seqs, CHUNK, p.shape[2]), row_map) for p in projs]
                 + [_full_spec(a) for a in consts],
        out_specs=[pl.BlockSpec((seqs, CHUNK, 2 * kw), row_map),
                   pl.BlockSpec((seqs, CONV_W - 1, 3 * kw), lambda b, c: (b, 0, 0)),
                   pl.BlockSpec((seqs, heads, HEAD_DIM, HEAD_DIM), lambda b, c: (b, 0, 0, 0)),
                   pl.BlockSpec((seqs, heads, HEAD_DIM, HEAD_DIM), lambda b, c: (b, 0, 0, 0))],
        out_shape=[jax.ShapeDtypeStruct((batch, seq_rows, 2 * kw), BF16),
                   jax.ShapeDtypeStruct((batch, CONV_W - 1, 3 * kw), F32),
                   jax.ShapeDtypeStruct((batch, heads, HEAD_DIM, HEAD_DIM), F32),
                   jax.ShapeDtypeStruct((batch, heads, HEAD_DIM, HEAD_DIM), F32)],
        scratch_shapes=[pltpu.VMEM((seqs, 8 + CHUNK, 3 * kw), F32),
                        pltpu.VMEM((seqs, heads, HEAD_DIM, HEAD_DIM), F32),
                        pltpu.VMEM((seqs, heads, HEAD_DIM, HEAD_DIM), F32)],
        compiler_params=_cparams(("arbitrary", "arbitrary")),
        name=f"mix_prompt_l{layer}",
    )(*projs, *consts)
    return mix.reshape(n_rows, 2 * kw), conv_new, gdn_new, hg_new


def _mix_sample(projs, conv_state, gdn_state, hgrn_state, params, layer, n_seq, toks, heads, carried):
    kw = heads * HEAD_DIM
    seqs = math.gcd(n_seq, SAMPLE_SEQS_PER_STEP)
    assert (seqs * toks) % 16 == 0
    sel_np, lev_np, n_levels = _block_constants(heads, toks)
    sel = jnp.asarray(sel_np, BF16)
    lev = jnp.asarray(lev_np)
    consts = params + (sel, lev)
    row_map = lambda n: (n, 0)
    kern = functools.partial(_mix_sample_kernel, heads=heads, layer=layer, n_levels=n_levels, toks=toks,
                             seqs=seqs)
    state_blk = pl.BlockSpec((1, seqs, heads, HEAD_DIM, HEAD_DIM), lambda n: (layer, n, 0, 0, 0))
    in_specs = [pl.BlockSpec((seqs * toks, p.shape[1]), row_map) for p in projs] + [
        pl.BlockSpec((1, seqs, CONV_W - 1, 3 * kw), lambda n: (layer, n, 0, 0)),
        state_blk, state_blk] + [_full_spec(a) for a in consts]
    args = tuple(projs) + (conv_state, gdn_state, hgrn_state) + consts
    aliases = {len(args): 2, len(args) + 1: 3}
    in_specs += [pl.BlockSpec(memory_space=pl.ANY), pl.BlockSpec(memory_space=pl.ANY)]
    args += tuple(carried)
    return pl.pallas_call(
        kern,
        grid=(n_seq // seqs,),
        in_specs=in_specs,
        out_specs=[pl.BlockSpec((seqs * toks, 2 * kw), row_map),
                   pl.BlockSpec((seqs, CONV_W - 1, 3 * kw), lambda n: (n, 0, 0)),
                   state_blk, state_blk],
        out_shape=[jax.ShapeDtypeStruct((n_seq * toks, 2 * kw), BF16),
                   jax.ShapeDtypeStruct((n_seq, CONV_W - 1, 3 * kw), F32),
                   jax.ShapeDtypeStruct(gdn_state.shape, F32),
                   jax.ShapeDtypeStruct(hgrn_state.shape, F32)],
        scratch_shapes=[pltpu.VMEM((seqs, 8 + toks, 3 * kw), F32)],
        input_output_aliases=aliases,
        compiler_params=_cparams(("arbitrary",)),
        name=f"mix_sample_l{layer}",
    )(*args)


def _rmsnorm_rows(x, w):
    return x * lax.rsqrt(jnp.mean(x * x, axis=-1, keepdims=True) + EPS) * w


def _norm_kernel(x_ref, w_ref, o_ref):
    o_ref[...] = _rmsnorm_rows(x_ref[...], w_ref[...]).astype(o_ref.dtype)


def _norm(x, w, out_dtype, block_rows, grid, in_map, out_rows, out_map, name):
    d = x.shape[1]
    return pl.pallas_call(
        _norm_kernel,
        grid=grid,
        in_specs=[pl.BlockSpec((block_rows, d), in_map), pl.BlockSpec((1, d), lambda *_: (0, 0))],
        out_specs=pl.BlockSpec((block_rows, d), out_map),
        out_shape=jax.ShapeDtypeStruct((out_rows, d), out_dtype),
        compiler_params=_cparams(("arbitrary",) * len(grid)),
        name=name,
    )(x, w[None, :])


def _norm_drop_lead(x, w, batch, lead, seq, out_dtype, name):
    d = x.shape[1]
    tr = _pick_tile(seq, 512)
    return pl.pallas_call(
        _norm_kernel,
        grid=(batch, seq // tr),
        in_specs=[pl.BlockSpec((pl.Element(1), pl.Element(tr), pl.Element(d)),
                               lambda b, r: (b, pl.multiple_of(lead + r * tr, 8), 0)),
                  pl.BlockSpec((1, d), lambda b, r: (0, 0))],
        out_specs=pl.BlockSpec((1, tr, d), lambda b, r: (b, r, 0)),
        out_shape=jax.ShapeDtypeStruct((batch, seq, d), out_dtype),
        compiler_params=_cparams(("arbitrary", "arbitrary")),
        name=name,
    )(x.reshape(batch, lead + seq, d), w[None, :])


def _proj_kernel(h_ref, wt_ref, o_ref, *zero_refs):
    o_ref[...] = lax.dot_general(h_ref[...], wt_ref[0].astype(BF16), (((1,), (1,)), ((), ())),
                                 preferred_element_type=F32)
    for z_ref in zero_refs:
        z_ref[...] = jnp.zeros_like(z_ref)


def _proj(h, w_t, layer, row0, n, name, zero_fill=None):
    m, d = h.shape
    tm = _pick_tile(m, 2112)
    tn = _pick_tile(n, 512, HEAD_DIM)
    assert row0 % 8 == 0
    grid = (m // tm, n // tn)
    out_specs = [pl.BlockSpec((tm, tn), lambda i, j: (i, j))]
    out_shape = [jax.ShapeDtypeStruct((m, n), F32)]
    if zero_fill is not None:
        slab_rows = math.prod(zero_fill) // HEAD_DIM // (grid[0] * grid[1])
        if slab_rows % 8 or slab_rows * HEAD_DIM * grid[0] * grid[1] != math.prod(zero_fill):
            return _proj(h, w_t, layer, row0, n, name), jnp.zeros(zero_fill, F32)
        out_specs.append(pl.BlockSpec((slab_rows, HEAD_DIM), lambda i, j: (i * grid[1] + j, 0)))
        out_shape.append(jax.ShapeDtypeStruct((slab_rows * grid[0] * grid[1], HEAD_DIM), F32))
    outs = pl.pallas_call(
        _proj_kernel,
        grid=grid,
        in_specs=[pl.BlockSpec((tm, d), lambda i, j: (i, 0)),
                  pl.BlockSpec((pl.Element(1), pl.Element(tn), pl.Element(d)),
                               lambda i, j: (layer, pl.multiple_of(row0 + j * tn, 8), 0))],
        out_specs=out_specs,
        out_shape=out_shape,
        compiler_params=_cparams(("arbitrary", "arbitrary")),
        name=name,
    )(h, w_t)
    if zero_fill is None:
        return outs[0]
    return outs[0], outs[1].reshape(zero_fill)


def _wout_kernel(x_ref, mix_ref, w_ref, nw_ref, x1_ref, h_ref):
    x1 = x_ref[...] + jnp.dot(mix_ref[...], w_ref[...], preferred_element_type=F32)
    x1_ref[...] = x1
    h_ref[...] = _rmsnorm_rows(x1, nw_ref[...]).astype(h_ref.dtype)


def _wout(x, mix, w_out, layer, norm_w, name):
    m, d = x.shape
    tm = _pick_tile(m, 352)
    return pl.pallas_call(
        _wout_kernel,
        grid=(m // tm,),
        in_specs=[pl.BlockSpec((tm, d), lambda i: (i, 0)),
                  pl.BlockSpec((tm, mix.shape[1]), lambda i: (i, 0)),
                  pl.BlockSpec((None,) + w_out.shape[1:], lambda i: (layer, 0, 0)),
                  pl.BlockSpec((1, d), lambda i: (0, 0))],
        out_specs=[pl.BlockSpec((tm, d), lambda i: (i, 0)), pl.BlockSpec((tm, d), lambda i: (i, 0))],
        out_shape=[jax.ShapeDtypeStruct((m, d), F32), jax.ShapeDtypeStruct((m, d), BF16)],
        compiler_params=_cparams(("arbitrary",)),
        name=name,
    )(x, mix, w_out, norm_w[None, :])


def _mlp_kernel(x_ref, h_ref, wup_ref, wdn_ref, nw_ref, xo_ref, ho_ref):
    f = pl.program_id(1)

    @pl.when(f == 0)
    def _():
        xo_ref[...] = x_ref[...]

    u = jnp.maximum(jnp.dot(h_ref[...], wup_ref[...], preferred_element_type=F32), 0.0)
    xo_ref[...] += jnp.dot((u * u).astype(BF16), wdn_ref[...], preferred_element_type=F32)

    @pl.when(f == pl.num_programs(1) - 1)
    def _():
        ho_ref[...] = _rmsnorm_rows(xo_ref[...], nw_ref[...]).astype(ho_ref.dtype)


def _mlp(x, h, w_up, w_down, layer, next_norm_w, name):
    m, d = x.shape
    ff = w_up.shape[2]
    tm = _pick_tile(m, 704)
    tf = _pick_tile(ff, 512, HEAD_DIM)
    return pl.pallas_call(
        _mlp_kernel,
        grid=(m // tm, ff // tf),
        in_specs=[pl.BlockSpec((tm, d), lambda i, f: (i, 0)),
                  pl.BlockSpec((tm, d), lambda i, f: (i, 0)),
                  pl.BlockSpec((None, d, tf), lambda i, f: (layer, 0, f)),
                  pl.BlockSpec((None, tf, d), lambda i, f: (layer, f, 0)),
                  pl.BlockSpec((1, d), lambda i, f: (0, 0))],
        out_specs=[pl.BlockSpec((tm, d), lambda i, f: (i, 0)), pl.BlockSpec((tm, d), lambda i, f: (i, 0))],
        out_shape=[jax.ShapeDtypeStruct((m, d), F32), jax.ShapeDtypeStruct((m, d), BF16)],
        compiler_params=_cparams(("arbitrary", "arbitrary")),
        name=name,
    )(x, h, w_up, w_down, next_norm_w[None, :])


def kernel(x_prompt, x_sample, state_gdn_conv, state_gdn, state_hgrn, meta_tokens, norm1_w, w_in, conv_w,
           a_log, dt_bias, gdn_norm_w, hgrn_lb, hgrn_norm_w, w_out, norm2_w, w_up, w_down, final_norm_w):
    batch, seq, d = x_prompt.shape
    n_seq, toks, _ = x_sample.shape
    depth = w_in.shape[0]
    heads = a_log.shape[1]
    kw = heads * HEAD_DIM
    n_meta = meta_tokens.shape[0]
    assert seq % CHUNK == 0 and 0 < n_meta <= CHUNK
    assert w_in.shape[2] == 8 * kw + 2 * heads and hgrn_lb.shape[1] == kw and 2 * heads <= HEAD_DIM

    w_in_t = jnp.swapaxes(w_in, 1, 2)
    w_ba_t = jnp.pad(w_in_t[:, 4 * kw:4 * kw + 2 * heads, :],
                     ((0, 0), (0, HEAD_DIM - 2 * heads), (0, 0))).astype(BF16)
    w_out_b = w_out.astype(BF16)
    w_up_b = w_up.astype(BF16)
    w_down_b = w_down.astype(BF16)

    lead = CHUNK - n_meta
    seq_rows = CHUNK + seq
    xp = jnp.concatenate([jnp.zeros((batch, lead, d), x_prompt.dtype),
                          jnp.broadcast_to(meta_tokens.astype(x_prompt.dtype)[None], (batch, n_meta, d)),
                          x_prompt], axis=1).reshape(batch * seq_rows, d)
    xs = x_sample.reshape(n_seq * toks, d)

    def first_norm(x, name):
        tm = _pick_tile(x.shape[0], 512)
        return _norm(x, norm1_w[0], BF16, tm, (x.shape[0] // tm,), lambda i: (i, 0), x.shape[0],
                     lambda i: (i, 0), name)

    hp = first_norm(xp, "norm_in_prompt")
    hs = first_norm(xs, "norm_in_sample")
    conv_p, gdn_p, hg_p, conv_s = [], [], [], []
    carried = None
    for l in range(depth):
        params = _mixer_params(l, heads, conv_w, a_log, dt_bias, gdn_norm_w, hgrn_norm_w, hgrn_lb)
        next_w = norm1_w[l + 1] if l + 1 < depth else final_norm_w
        pg = _proj(hp, w_in_t, l, 0, 4 * kw, f"proj_gdn_prompt_l{l}",
                   zero_fill=state_gdn.shape if l == 0 else None)
        ph = _proj(hp, w_in_t, l, 4 * kw + 2 * heads, 4 * kw, f"proj_hgrn_prompt_l{l}",
                   zero_fill=state_hgrn.shape if l == 0 else None)
        if l == 0:
            (pg, acc_g), (ph, acc_h) = pg, ph
            carried = (acc_g, acc_h)
        projs = (pg, ph, _proj(hp, w_ba_t, l, 0, HEAD_DIM, f"proj_ba_prompt_l{l}"))
        mix, c_new, g_new, h_new = _mix_prompt(projs, params, l, batch, heads)
        x1, h2 = _wout(xp, mix, w_out_b, l, norm2_w[l], f"wout_prompt_l{l}")
        xp, hp = _mlp(x1, h2, w_up_b, w_down_b, l, next_w, f"mlp_prompt_l{l}")
        conv_p.append(c_new); gdn_p.append(g_new); hg_p.append(h_new)
        projs = (_proj(hs, w_in_t, l, 0, 4 * kw, f"proj_gdn_sample_l{l}"),
                 _proj(hs, w_in_t, l, 4 * kw + 2 * heads, 4 * kw, f"proj_hgrn_sample_l{l}"),
                 _proj(hs, w_ba_t, l, 0, HEAD_DIM, f"proj_ba_sample_l{l}"))
        mix, c_new, g_new, h_new = _mix_sample(projs, state_gdn_conv, state_gdn, state_hgrn, params, l,
                                               n_seq, toks, heads, carried)
        carried = (g_new, h_new)
        x1, h2 = _wout(xs, mix, w_out_b, l, norm2_w[l], f"wout_sample_l{l}")
        xs, hs = _mlp(x1, h2, w_up_b, w_down_b, l, next_w, f"mlp_sample_l{l}")
        conv_s.append(c_new)

    yp = _norm_drop_lead(xp, final_norm_w, batch, CHUNK, seq, x_prompt.dtype, "norm_out_prompt")
    tm = _pick_tile(n_seq * toks, 512)
    ys = _norm(xs, final_norm_w, x_sample.dtype, tm, (n_seq * toks // tm,), lambda i: (i, 0), n_seq * toks,
               lambda i: (i, 0), "norm_out_sample").reshape(n_seq, toks, d)
    return (yp, ys, jnp.stack(conv_p).astype(x_prompt.dtype), jnp.stack(gdn_p).astype(state_gdn.dtype),
            jnp.stack(hg_p).astype(state_hgrn.dtype), jnp.stack(conv_s).astype(state_gdn_conv.dtype),
            carried[0].astype(state_gdn.dtype), carried[1].astype(state_hgrn.dtype))
```

```python
import functools
import math

import numpy as np
import jax
import jax.numpy as jnp
from jax import lax
from jax.experimental import pallas as pl
from jax.experimental.pallas import tpu as pltpu

HEAD_DIM = 128
CHUNK = 64
CONV_W = 4
EPS = 1e-6
SAMPLE_SEQS_PER_STEP = 8
PROMPT_SEQS_PER_STEP = 2
V7X_VMEM_LIMIT_BYTES = 56 * 1024 * 1024

F32 = jnp.float32
BF16 = jnp.bfloat16


def _mm(a, b):
    return jnp.dot(a.astype(BF16), b.astype(BF16), preferred_element_type=F32)


def _mm_nt(a, b):
    return lax.dot_general(a.astype(BF16), b.astype(BF16), (((1,), (1,)), ((), ())),
                           preferred_element_type=F32)


def _mm_tn(a, b):
    return lax.dot_general(a.astype(BF16), b.astype(BF16), (((0,), (0,)), ((), ())),
                           preferred_element_type=F32)


def _split_bf16(x):
    hi = x.astype(BF16)
    return hi, (x - hi.astype(F32)).astype(BF16)


def _mm_3pass(a, b):
    ah, al = _split_bf16(a)
    bh, bl = _split_bf16(b)
    return (jnp.dot(ah, bh, preferred_element_type=F32) + jnp.dot(ah, bl, preferred_element_type=F32)
            + jnp.dot(al, bh, preferred_element_type=F32))


def _mm_sel(m01, x):
    x1 = x.astype(BF16)
    r1 = x - x1.astype(F32)
    x2 = r1.astype(BF16)
    x3 = (r1 - x2.astype(F32)).astype(BF16)
    return (jnp.dot(m01, x1, preferred_element_type=F32)
            + jnp.dot(m01, x2, preferred_element_type=F32)
            + jnp.dot(m01, x3, preferred_element_type=F32))


def _sigmoid(x):
    return 1.0 / (1.0 + jnp.exp(-x))


def _sigmoid_abs(x):
    return 0.5 * jnp.tanh(0.5 * x) + 0.5


def _silu(x):
    h = 0.5 * x
    return h + h * jnp.tanh(h)


def _softplus(x):
    return jnp.maximum(x, 0.0) + jnp.log(1.0 + jnp.exp(-jnp.abs(x)))


def _pick_tile(m, target, mult=16):
    best = None
    for t in range(mult, m + 1, mult):
        if m % t == 0 and t <= target:
            best = t
    if best is None:
        raise ValueError(f"no tile for {m}")
    return best


def _cparams(sem):
    return pltpu.CompilerParams(dimension_semantics=sem, vmem_limit_bytes=V7X_VMEM_LIMIT_BYTES)


@functools.lru_cache(maxsize=None)
def _block_constants(groups, toks):
    rows = groups * toks
    grp = np.arange(rows) // toks
    pos = np.arange(rows) % toks
    same = grp[:, None] == grp[None, :]
    pt, ps = pos[:, None], pos[None, :]
    prefix = (same & (ps <= pt)).astype(np.float32)
    lev = np.full((rows, rows), -1, np.int32)
    lev[np.arange(rows), np.arange(rows)] = 0
    n, idx = toks, 1
    while n >= 2:
        half = n // 2
        blk = same & ((pt // n) == (ps // n))
        lev[blk & (pt % n >= half) & (ps % n < half)] = idx
        n //= 2
        idx += 1
    return prefix, lev, idx - 1


def _rows_from(b, offset, n):
    rows = b.shape[0]
    if n % 8 == 0:
        return jnp.concatenate(
            [jnp.broadcast_to(b[s + offset:s + offset + 1, :], (n, HEAD_DIM)) for s in range(0, rows, n)], axis=0)
    b3 = b.reshape(rows // 8, 8, HEAD_DIM)
    sub = lax.broadcasted_iota(jnp.int32, b3.shape, 1)
    out = None
    for s in range(0, 8, n):
        cand = jnp.broadcast_to(b3[:, s + offset:s + offset + 1, :], b3.shape)
        out = cand if out is None else jnp.where(sub >= s, cand, out)
    return out.reshape(rows, HEAD_DIM)


def _gdn_blocks(qs, ks, vs, betas, bcols, brows, sufcols, lev, states, toks):
    blocks = range(len(qs))
    rows = qs[0].shape[0]
    groups = range(rows // toks)
    gsl = [slice(g * toks, (g + 1) * toks) for g in groups]
    eye = (lev == 0).astype(F32)
    decay = [jnp.exp(jnp.where(lev >= 0, bcols[p] - brows[p], -jnp.inf)) for p in blocks]
    kb = [ks[p] * betas[p] for p in blocks]
    a = [jnp.where(lev >= 1, _mm_nt(kb[p], ks[p]) * decay[p], 0.0) for p in blocks]
    eb = [jnp.exp(bcols[p]) for p in blocks]
    qe = [qs[p] * eb[p] for p in blocks]
    o_state = [jnp.concatenate([_mm(qe[p][gsl[g]], states[p][g]) for g in groups], axis=0) for p in blocks]
    qk = [_mm_nt(qs[p], ks[p]) * decay[p] for p in blocks]
    inv = [eye - a[p] for p in blocks]
    apow = a
    for _ in range(int(math.log2(toks)) - 1):
        apow = [_mm_3pass(x, x) for x in apow]
        inv = [inv[p] + _mm_3pass(inv[p], apow[p]) for p in blocks]
    sol = [_mm_3pass(inv[p], jnp.concatenate([vs[p] * betas[p], kb[p] * eb[p]], axis=1)) for p in blocks]
    v_new = [sol[p][:, :HEAD_DIM]
             - jnp.concatenate([_mm(sol[p][gsl[g], HEAD_DIM:], states[p][g]) for g in groups], axis=0)
             for p in blocks]
    o = [o_state[p] + _mm(qk[p], v_new[p]) for p in blocks]
    kdec = [ks[p] * jnp.exp(sufcols[p]) for p in blocks]
    new_states = []
    for p in blocks:
        per_group = []
        for g in groups:
            r0 = g * toks
            ebl = jnp.exp(bcols[p][r0:r0 + 1] + sufcols[p][r0:r0 + 1])
            per_group.append(ebl * states[p][g] + _mm_tn(kdec[p][gsl[g]], v_new[p][gsl[g]]))
        new_states.append(per_group)
    return o, new_states


def _hgrn_blocks(qs, f_logits, vs, lbs, prefix, lev, n_levels, states_t, toks):
    blocks = range(len(qs))
    rows = qs[0].shape[0]
    groups = range(rows // toks)
    gsl = [slice(g * toks, (g + 1) * toks) for g in groups]
    fg = [lbs[p] + (1.0 - lbs[p]) * _sigmoid(f_logits[p]) for p in blocks]
    ks = [1.0 - fg[p] for p in blocks]
    b = [_mm_sel(prefix, jnp.log(fg[p])) for p in blocks]
    x_pre = [jnp.exp(b[p]) for p in blocks]
    x_suf = [jnp.exp(-jnp.abs(b[p] - _rows_from(b[p], toks - 1, toks))) for p in blocks]
    a = [jnp.where(lev == 0, _mm_nt(qs[p], ks[p]), 0.0) for p in blocks]
    for i in range(n_levels):
        n = toks >> i
        xl = [jnp.exp(-jnp.abs(b[p] - _rows_from(b[p], n // 2 - 1, n))) for p in blocks]
        a = [a[p] + jnp.where(lev == i + 1, _mm_nt(qs[p] * xl[p], ks[p] * xl[p]), 0.0) for p in blocks]
    o = [jnp.concatenate([_mm_nt((qs[p] * x_pre[p])[gsl[g]], states_t[p][g]) for g in groups], axis=0)
         + _mm(a[p], vs[p]) for p in blocks]
    new_states = []
    for p in blocks:
        kdec = ks[p] * x_suf[p]
        per_group = []
        for g in groups:
            last = g * toks + toks - 1
            per_group.append(states_t[p][g] * x_pre[p][last:last + 1] + _mm_tn(vs[p][gsl[g]], kdec[gsl[g]]))
        new_states.append(per_group)
    return o, new_states


def _head_norm_gate(o, z, w):
    y = o * lax.rsqrt(jnp.mean(o * o, axis=-1, keepdims=True) + EPS)
    return y * w * _silu(z)


def _l2norm(x):
    return x * lax.rsqrt(jnp.sum(x * x, axis=-1, keepdims=True) + EPS)


def _layer_lower_bound(lbp, layer):
    m = jnp.max(lbp, axis=0, keepdims=True)
    e = jnp.exp(lbp - m)
    p = e / jnp.sum(e, axis=0, keepdims=True)
    if layer == 0:
        return jnp.zeros_like(m)
    return jnp.sum(p[1:layer + 1], axis=0, keepdims=True)


def _conv_slab(cbuf, base, rows, cols, w):
    y = (w[3:4] * cbuf[base:base + rows, cols] + w[2:3] * cbuf[base - 1:base - 1 + rows, cols]
         + w[1:2] * cbuf[base - 2:base - 2 + rows, cols] + w[0:1] * cbuf[base - 3:base - 3 + rows, cols])
    return _silu(y)


def _mix_prompt_kernel(pg_ref, ph_ref, ba_ref, convw_ref, alog_ref, dtb_ref, gnw_ref, hnw_ref, lbp_ref,
                       sel_ref, lev_ref, mix_ref, convo_ref, sgo_ref, sho_ref,
                       cbuf, sg, sht, *, heads, layer, n_levels, seqs):
    c = pl.program_id(1)
    kw = heads * HEAD_DIM
    rows = CHUNK

    @pl.when(c == 0)
    def _():
        cbuf[:, 0:8, :] = jnp.zeros((seqs, 8, 3 * kw), F32)
        sg[...] = jnp.zeros_like(sg)
        sht[...] = jnp.zeros_like(sht)

    sel = sel_ref[...]
    lev = lev_ref[...]
    gnw = gnw_ref[...]
    hnw = hnw_ref[...]
    scale = HEAD_DIM ** -0.5
    ps = [(s, h) for s in range(seqs) for h in range(heads)]

    def slab(off, h):
        return slice(off + h * HEAD_DIM, off + (h + 1) * HEAD_DIM)

    def conv(s, off, h):
        return _conv_slab(cbuf.at[s], 8, rows, slab(off, h), convw_ref[:, slab(off, h)])

    for s in range(seqs):
        cbuf[s, 8:8 + rows, :] = pg_ref[s, :, 0:3 * kw]
    qs = [_l2norm(conv(s, 0, h)) * scale for s, h in ps]
    ks = [_l2norm(conv(s, kw, h)) for s, h in ps]
    vs = [conv(s, 2 * kw, h) for s, h in ps]
    for s in range(seqs):
        tail = cbuf[s, 8 + rows - 3:8 + rows, :]
        cbuf[s, 5:8, :] = tail
        convo_ref[s] = tail

    beta_all, b_all, b_all_t, suf_all = [], [], [], []
    for s in range(seqs):
        ba = ba_ref[s]
        beta_all.append(_sigmoid_abs(ba))
        g_all = -jnp.exp(alog_ref[...]) * _softplus(ba + dtb_ref[...])
        b = _mm_sel(sel, g_all)
        b_all.append(b)
        b_all_t.append(b.T)
        suf_all.append(b[rows - 1:rows, :] - b)
    os_, new = _gdn_blocks(qs, ks, vs, [beta_all[s][:, h:h + 1] for s, h in ps],
                           [b_all[s][:, heads + h:heads + h + 1] for s, h in ps],
                           [b_all_t[s][heads + h:heads + h + 1, :] for s, h in ps],
                           [suf_all[s][:, heads + h:heads + h + 1] for s, h in ps], lev,
                           [[sg[s, h]] for s, h in ps], rows)
    for p, (s, h) in enumerate(ps):
        sg[s, h] = new[p][0]
        mix_ref[s, :, slab(0, h)] = _head_norm_gate(os_[p], pg_ref[s, :, slab(3 * kw, h)],
                                                    gnw).astype(mix_ref.dtype)

    lb_all = _layer_lower_bound(lbp_ref[...], layer)
    os_, new = _hgrn_blocks([ph_ref[s, :, slab(0, h)] for s, h in ps],
                            [ph_ref[s, :, slab(kw, h)] for s, h in ps],
                            [ph_ref[s, :, slab(2 * kw, h)] for s, h in ps],
                            [lb_all[:, slab(0, h)] for s, h in ps], sel, lev, n_levels,
                            [[sht[s, h]] for s, h in ps], rows)
    for p, (s, h) in enumerate(ps):
        sht[s, h] = new[p][0]
        mix_ref[s, :, slab(kw, h)] = _head_norm_gate(os_[p], ph_ref[s, :, slab(3 * kw, h)],
                                                     hnw).astype(mix_ref.dtype)

    @pl.when(c == pl.num_programs(1) - 1)
    def _():
        sgo_ref[...] = sg[...]
        for s, h in ps:
            sho_ref[s, h] = sht[s, h].T


def _mix_sample_kernel(pg_ref, ph_ref, ba_ref, convs_ref, sgi_ref, shi_ref, convw_ref, alog_ref, dtb_ref,
                       gnw_ref, hnw_ref, lbp_ref, sel_ref, lev_ref, *rest,
                       heads, layer, n_levels, toks, seqs):
    mix_ref, convo_ref, sgo_ref, sho_ref, cbuf = rest[-5:]
    kw = heads * HEAD_DIM
    sel = sel_ref[...]
    lev = lev_ref[...]
    rows = heads * toks
    gnw = gnw_ref[...]
    hnw = hnw_ref[...]
    scale = HEAD_DIM ** -0.5
    lb_all = _layer_lower_bound(lbp_ref[...], layer)
    lb_rows = jnp.concatenate(
        [jnp.broadcast_to(lb_all[:, h * HEAD_DIM:(h + 1) * HEAD_DIM], (toks, HEAD_DIM)) for h in range(heads)],
        axis=0)

    ss = range(seqs)
    hs = range(heads)
    trs = [slice(s * toks, (s + 1) * toks) for s in ss]

    def slab(off, h):
        return slice(off + h * HEAD_DIM, off + (h + 1) * HEAD_DIM)

    def stack(ref, s, off):
        return jnp.concatenate([ref[trs[s], slab(off, h)] for h in hs], axis=0)

    def conv_stack(s, off):
        return jnp.concatenate(
            [_conv_slab(cbuf.at[s], 8, toks, slab(off, h), convw_ref[:, slab(off, h)]) for h in hs], axis=0)

    for s in ss:
        cbuf[s, 5:8, :] = convs_ref[0, s]
        cbuf[s, 8:8 + toks, :] = pg_ref[trs[s], 0:3 * kw]
    qs = [_l2norm(conv_stack(s, 0)) * scale for s in ss]
    ks = [_l2norm(conv_stack(s, kw)) for s in ss]
    vs = [conv_stack(s, 2 * kw) for s in ss]
    for s in ss:
        convo_ref[s] = cbuf[s, 8 + toks - 3:8 + toks, :]

    betas, bcols, brows, sufcols = [], [], [], []
    for s in ss:
        ba = ba_ref[trs[s], :]
        beta_all = _sigmoid_abs(ba)
        g_all = -jnp.exp(alog_ref[...]) * _softplus(ba + dtb_ref[...])
        betas.append(jnp.concatenate([beta_all[:, h:h + 1] for h in hs], axis=0))
        gcol = jnp.concatenate([g_all[:, heads + h:heads + h + 1] for h in hs], axis=0)
        b_b = _mm_sel(sel, jnp.broadcast_to(gcol, (rows, HEAD_DIM)))
        bcols.append(b_b[:, 0:1])
        brows.append(b_b.T[0:1, :])
        sufcols.append((_rows_from(b_b, toks - 1, toks) - b_b)[:, 0:1])
    os_, new = _gdn_blocks(qs, ks, vs, betas, bcols, brows, sufcols, lev,
                           [[sgi_ref[0, s, h] for h in hs] for s in ss], toks)
    for s in ss:
        og = _head_norm_gate(os_[s], stack(pg_ref, s, 3 * kw), gnw)
        for h in hs:
            sgo_ref[0, s, h] = new[s][h]
            mix_ref[trs[s], slab(0, h)] = og[h * toks:(h + 1) * toks].astype(mix_ref.dtype)

    os_, new = _hgrn_blocks([stack(ph_ref, s, 0) for s in ss], [stack(ph_ref, s, kw) for s in ss],
                            [stack(ph_ref, s, 2 * kw) for s in ss], [lb_rows for s in ss], sel, lev, n_levels,
                            [[shi_ref[0, s, h].T for h in hs] for s in ss], toks)
    for s in ss:
        oh = _head_norm_gate(os_[s], stack(ph_ref, s, 3 * kw), hnw)
        for h in hs:
            sho_ref[0, s, h] = new[s][h].T
            mix_ref[trs[s], slab(kw, h)] = oh[h * toks:(h + 1) * toks].astype(mix_ref.dtype)


def _mixer_params(layer, heads, conv_w, a_log, dt_bias, gdn_norm_w, hgrn_norm_w, hgrn_lb):
    pad = HEAD_DIM - 2 * heads
    alog = jnp.pad(a_log[layer][None, :], ((0, 0), (heads, pad)))
    dtb = jnp.pad(dt_bias[layer][None, :], ((0, 0), (heads, pad)))
    return (conv_w[layer], alog, dtb, gdn_norm_w[layer][None, :], hgrn_norm_w[layer][None, :], hgrn_lb)


def _full_spec(a):
    nd = a.ndim
    return pl.BlockSpec(a.shape, lambda *_: (0,) * nd)


def _mix_prompt(projs, params, layer, batch, heads):
    n_rows = projs[0].shape[0]
    seq_rows = n_rows // batch
    seqs = PROMPT_SEQS_PER_STEP if batch % PROMPT_SEQS_PER_STEP == 0 else 1
    kw = heads * HEAD_DIM
    sel_np, lev_np, n_levels = _block_constants(1, CHUNK)
    sel = jnp.asarray(sel_np, BF16)
    lev = jnp.asarray(lev_np)
    row_map = lambda b, c: (b, c, 0)
    consts = params + (sel, lev)
    projs = [p.reshape(batch, seq_rows, p.shape[1]) for p in projs]
    kern = functools.partial(_mix_prompt_kernel, heads=heads, layer=layer, n_levels=n_levels, seqs=seqs)
    mix, conv_new, gdn_new, hg_new = pl.pallas_call(
        kern,
        grid=(batch // seqs, seq_rows // CHUNK),
        in_specs=[pl.BlockSpec((seqs, CHUNK, p.shape[2]), row_map) for p in projs]
                 + [_full_spec(a) for a in consts],
        out_specs=[pl.BlockSpec((seqs, CHUNK, 2 * kw), row_map),
                   pl.BlockSpec((seqs, CONV_W - 1, 3 * kw), lambda b, c: (b, 0, 0)),
                   pl.BlockSpec((seqs, heads, HEAD_DIM, HEAD_DIM), lambda b, c: (b, 0, 0, 0)),
                   pl.BlockSpec((seqs, heads, HEAD_DIM, HEAD_DIM), lambda b, c: (b, 0, 0, 0))],
        out_shape=[jax.ShapeDtypeStruct((batch, seq_rows, 2 * kw), BF16),
                   jax.ShapeDtypeStruct((batch, CONV_W - 1, 3 * kw), F32),
                   jax.ShapeDtypeStruct((batch, heads, HEAD_DIM, HEAD_DIM), F32),
                   jax.ShapeDtypeStruct((batch, heads, HEAD_DIM, HEAD_DIM), F32)],
        scratch_shapes=[pltpu.VMEM((seqs, 8 + CHUNK, 3 * kw), F32),
                        pltpu.VMEM((seqs, heads, HEAD_DIM, HEAD_DIM), F32),
                        pltpu.VMEM((seqs, heads, HEAD_DIM, HEAD_DIM), F32)],
        compiler_params=_cparams(("arbitrary", "arbitrary")),
        name=f"mix_prompt_l{layer}",
    )(*projs, *consts)
    return mix.reshape(n_rows, 2 * kw), conv_new, gdn_new, hg_new


def _mix_sample(projs, conv_state, gdn_state, hgrn_state, params, layer, n_seq, toks, heads, carried):
    kw = heads * HEAD_DIM
    seqs = math.gcd(n_seq, SAMPLE_SEQS_PER_STEP)
    assert (seqs * toks) % 16 == 0
    sel_np, lev_np, n_levels = _block_constants(heads, toks)
    sel = jnp.asarray(sel_np, BF16)
    lev = jnp.asarray(lev_np)
    consts = params + (sel, lev)
    row_map = lambda n: (n, 0)
    kern = functools.partial(_mix_sample_kernel, heads=heads, layer=layer, n_levels=n_levels, toks=toks,
                             seqs=seqs)
    state_blk = pl.BlockSpec((1, seqs, heads, HEAD_DIM, HEAD_DIM), lambda n: (layer, n, 0, 0, 0))
    in_specs = [pl.BlockSpec((seqs * toks, p.shape[1]), row_map) for p in projs] + [
        pl.BlockSpec((1, seqs, CONV_W - 1, 3 * kw), lambda n: (layer, n, 0, 0)),
        state_blk, state_blk] + [_full_spec(a) for a in consts]
    args = tuple(projs) + (conv_state, gdn_state, hgrn_state) + consts
    aliases = {len(args): 2, len(args) + 1: 3}
    in_specs += [pl.BlockSpec(memory_space=pl.ANY), pl.BlockSpec(memory_space=pl.ANY)]
    args += tuple(carried)
    return pl.pallas_call(
        kern,
        grid=(n_seq // seqs,),
        in_specs=in_specs,
        out_specs=[pl.BlockSpec((seqs * toks, 2 * kw), row_map),
                   pl.BlockSpec((seqs, CONV_W - 1, 3 * kw), lambda n: (n, 0, 0)),
                   state_blk, state_blk],
        out_shape=[jax.ShapeDtypeStruct((n_seq * toks, 2 * kw), BF16),
                   jax.ShapeDtypeStruct((n_seq, CONV_W - 1, 3 * kw), F32),
                   jax.ShapeDtypeStruct(gdn_state.shape, F32),
                   jax.ShapeDtypeStruct(hgrn_state.shape, F32)],
        scratch_shapes=[pltpu.VMEM((seqs, 8 + toks, 3 * kw), F32)],
        input_output_aliases=aliases,
        compiler_params=_cparams(("arbitrary",)),
        name=f"mix_sample_l{layer}",
    )(*args)


def _rmsnorm_rows(x, w):
    return x * lax.rsqrt(jnp.mean(x * x, axis=-1, keepdims=True) + EPS) * w


def _norm_kernel(x_ref, w_ref, o_ref):
    o_ref[...] = _rmsnorm_rows(x_ref[...], w_ref[...]).astype(o_ref.dtype)


def _norm(x, w, out_dtype, block_rows, grid, in_map, out_rows, out_map, name):
    d = x.shape[1]
    return pl.pallas_call(
        _norm_kernel,
        grid=grid,
        in_specs=[pl.BlockSpec((block_rows, d), in_map), pl.BlockSpec((1, d), lambda *_: (0, 0))],
        out_specs=pl.BlockSpec((block_rows, d), out_map),
        out_shape=jax.ShapeDtypeStruct((out_rows, d), out_dtype),
        compiler_params=_cparams(("arbitrary",) * len(grid)),
        name=name,
    )(x, w[None, :])


def _norm_drop_lead(x, w, batch, lead, seq, out_dtype, name):
    d = x.shape[1]
    tr = _pick_tile(seq, 512)
    return pl.pallas_call(
        _norm_kernel,
        grid=(batch, seq // tr),
        in_specs=[pl.BlockSpec((pl.Element(1), pl.Element(tr), pl.Element(d)),
                               lambda b, r: (b, pl.multiple_of(lead + r * tr, 8), 0)),
                  pl.BlockSpec((1, d), lambda b, r: (0, 0))],
        out_specs=pl.BlockSpec((1, tr, d), lambda b, r: (b, r, 0)),
        out_shape=jax.ShapeDtypeStruct((batch, seq, d), out_dtype),
        compiler_params=_cparams(("arbitrary", "arbitrary")),
        name=name,
    )(x.reshape(batch, lead + seq, d), w[None, :])


def _proj_kernel(h_ref, wt_ref, o_ref, *zero_refs):
    o_ref[...] = lax.dot_general(h_ref[...], wt_ref[0].astype(BF16), (((1,), (1,)), ((), ())),
                                 preferred_element_type=F32)
    for z_ref in zero_refs:
        z_ref[...] = jnp.zeros_like(z_ref)


def _proj(h, w_t, layer, row0, n, name, zero_fill=None):
    m, d = h.shape
    tm = _pick_tile(m, 2112)
    tn = _pick_tile(n, 512, HEAD_DIM)
    assert row0 % 8 == 0
    grid = (m // tm, n // tn)
    out_specs = [pl.BlockSpec((tm, tn), lambda i, j: (i, j))]
    out_shape = [jax.ShapeDtypeStruct((m, n), F32)]
    if zero_fill is not None:
        slab_rows = math.prod(zero_fill) // HEAD_DIM // (grid[0] * grid[1])
        if slab_rows % 8 or slab_rows * HEAD_DIM * grid[0] * grid[1] != math.prod(zero_fill):
            return _proj(h, w_t, layer, row0, n, name), jnp.zeros(zero_fill, F32)
        out_specs.append(pl.BlockSpec((slab_rows, HEAD_DIM), lambda i, j: (i * grid[1] + j, 0)))
        out_shape.append(jax.ShapeDtypeStruct((slab_rows * grid[0] * grid[1], HEAD_DIM), F32))
    outs = pl.pallas_call(
        _proj_kernel,
        grid=grid,
        in_specs=[pl.BlockSpec((tm, d), lambda i, j: (i, 0)),
                  pl.BlockSpec((pl.Element(1), pl.Element(tn), pl.Element(d)),
                               lambda i, j: (layer, pl.multiple_of(row0 + j * tn, 8), 0))],
        out_specs=out_specs,
        out_shape=out_shape,
        compiler_params=_cparams(("arbitrary", "arbitrary")),
        name=name,
    )(h, w_t)
    if zero_fill is None:
        return outs[0]
    return outs[0], outs[1].reshape(zero_fill)


def _wout_kernel(x_ref, mix_ref, w_ref, nw_ref, x1_ref, h_ref):
    x1 = x_ref[...] + jnp.dot(mix_ref[...], w_ref[...], preferred_element_type=F32)
    x1_ref[...] = x1
    h_ref[...] = _rmsnorm_rows(x1, nw_ref[...]).astype(h_ref.dtype)


def _wout(x, mix, w_out, layer, norm_w, name):
    m, d = x.shape
    tm = _pick_tile(m, 704)
    return pl.pallas_call(
        _wout_kernel,
        grid=(m // tm,),
        in_specs=[pl.BlockSpec((tm, d), lambda i: (i, 0)),
                  pl.BlockSpec((tm, mix.shape[1]), lambda i: (i, 0)),
                  pl.BlockSpec((None,) + w_out.shape[1:], lambda i: (layer, 0, 0),
                               pipeline_mode=pl.Buffered(1)),
                  pl.BlockSpec((1, d), lambda i: (0, 0))],
        out_specs=[pl.BlockSpec((tm, d), lambda i: (i, 0)), pl.BlockSpec((tm, d), lambda i: (i, 0))],
        out_shape=[jax.ShapeDtypeStruct((m, d), F32), jax.ShapeDtypeStruct((m, d), BF16)],
        compiler_params=_cparams(("arbitrary",)),
        name=name,
    )(x, mix, w_out, norm_w[None, :])


def _mlp_kernel(x_ref, h_ref, wup_ref, wdn_ref, nw_ref, xo_ref, ho_ref):
    f = pl.program_id(1)

    @pl.when(f == 0)
    def _():
        xo_ref[...] = x_ref[...]

    u = jnp.maximum(jnp.dot(h_ref[...], wup_ref[...], preferred_element_type=F32), 0.0)
    xo_ref[...] += jnp.dot((u * u).astype(BF16), wdn_ref[...], preferred_element_type=F32)

    @pl.when(f == pl.num_programs(1) - 1)
    def _():
        ho_ref[...] = _rmsnorm_rows(xo_ref[...], nw_ref[...]).astype(ho_ref.dtype)


def _mlp(x, h, w_up, w_down, layer, next_norm_w, name):
    m, d = x.shape
    ff = w_up.shape[2]
    tm = _pick_tile(m, 704)
    tf = _pick_tile(ff, 512, HEAD_DIM)
    return pl.pallas_call(
        _mlp_kernel,
        grid=(m // tm, ff // tf),
        in_specs=[pl.BlockSpec((tm, d), lambda i, f: (i, 0)),
                  pl.BlockSpec((tm, d), lambda i, f: (i, 0)),
                  pl.BlockSpec((None, d, tf), lambda i, f: (layer, 0, f)),
                  pl.BlockSpec((None, tf, d), lambda i, f: (layer, f, 0)),
                  pl.BlockSpec((1, d), lambda i, f: (0, 0))],
        out_specs=[pl.BlockSpec((tm, d), lambda i, f: (i, 0)), pl.BlockSpec((tm, d), lambda i, f: (i, 0))],
        out_shape=[jax.ShapeDtypeStruct((m, d), F32), jax.ShapeDtypeStruct((m, d), BF16)],
        compiler_params=_cparams(("arbitrary", "arbitrary")),
        name=name,
    )(x, h, w_up, w_down, next_norm_w[None, :])


def kernel(x_prompt, x_sample, state_gdn_conv, state_gdn, state_hgrn, meta_tokens, norm1_w, w_in, conv_w,
           a_log, dt_bias, gdn_norm_w, hgrn_lb, hgrn_norm_w, w_out, norm2_w, w_up, w_down, final_norm_w):
    batch, seq, d = x_prompt.shape
    n_seq, toks, _ = x_sample.shape
    depth = w_in.shape[0]
    heads = a_log.shape[1]
    kw = heads * HEAD_DIM
    n_meta = meta_tokens.shape[0]
    assert seq % CHUNK == 0 and 0 < n_meta <= CHUNK
    assert w_in.shape[2] == 8 * kw + 2 * heads and hgrn_lb.shape[1] == kw and 2 * heads <= HEAD_DIM

    w_in_t = jnp.swapaxes(w_in, 1, 2)
    w_ba_t = jnp.pad(w_in_t[:, 4 * kw:4 * kw + 2 * heads, :],
                     ((0, 0), (0, HEAD_DIM - 2 * heads), (0, 0))).astype(BF16)
    w_out_b = w_out.astype(BF16)
    w_up_b = w_up.astype(BF16)
    w_down_b = w_down.astype(BF16)

    lead = CHUNK - n_meta
    seq_rows = CHUNK + seq
    xp = jnp.concatenate([jnp.zeros((batch, lead, d), x_prompt.dtype),
                          jnp.broadcast_to(meta_tokens.astype(x_prompt.dtype)[None], (batch, n_meta, d)),
                          x_prompt], axis=1).reshape(batch * seq_rows, d)
    xs = x_sample.reshape(n_seq * toks, d)

    def first_norm(x, name):
        tm = _pick_tile(x.shape[0], 512)
        return _norm(x, norm1_w[0], BF16, tm, (x.shape[0] // tm,), lambda i: (i, 0), x.shape[0],
                     lambda i: (i, 0), name)

    hp = first_norm(xp, "norm_in_prompt")
    hs = first_norm(xs, "norm_in_sample")
    conv_p, gdn_p, hg_p, conv_s = [], [], [], []
    carried = None
    for l in range(depth):
        params = _mixer_params(l, heads, conv_w, a_log, dt_bias, gdn_norm_w, hgrn_norm_w, hgrn_lb)
        next_w = norm1_w[l + 1] if l + 1 < depth else final_norm_w
        pg = _proj(hp, w_in_t, l, 0, 4 * kw, f"proj_gdn_prompt_l{l}",
                   zero_fill=state_gdn.shape if l == 0 else None)
        ph = _proj(hp, w_in_t, l, 4 * kw + 2 * heads, 4 * kw, f"proj_hgrn_prompt_l{l}",
                   zero_fill=state_hgrn.shape if l == 0 else None)
        if l == 0:
            (pg, acc_g), (ph, acc_h) = pg, ph
            carried = (acc_g, acc_h)
        projs = (pg, ph, _proj(hp, w_ba_t, l, 0, HEAD_DIM, f"proj_ba_prompt_l{l}"))
        mix, c_new, g_new, h_new = _mix_prompt(projs, params, l, batch, heads)
        x1, h2 = _wout(xp, mix, w_out_b, l, norm2_w[l], f"wout_prompt_l{l}")
        xp, hp = _mlp(x1, h2, w_up_b, w_down_b, l, next_w, f"mlp_prompt_l{l}")
        conv_p.append(c_new); gdn_p.append(g_new); hg_p.append(h_new)
        projs = (_proj(hs, w_in_t, l, 0, 4 * kw, f"proj_gdn_sample_l{l}"),
                 _proj(hs, w_in_t, l, 4 * kw + 2 * heads, 4 * kw, f"proj_hgrn_sample_l{l}"),
                 _proj(hs, w_ba_t, l, 0, HEAD_DIM, f"proj_ba_sample_l{l}"))
        mix, c_new, g_new, h_new = _mix_sample(projs, state_gdn_conv, state_gdn, state_hgrn, params, l,
                                               n_seq, toks, heads, carried)
        carried = (g_new, h_new)
        x1, h2 = _wout(xs, mix, w_out_b, l, norm2_w[l], f"wout_sample_l{l}")
        xs, hs = _mlp(x1, h2, w_up_b, w_down_b, l, next_w, f"mlp_sample_l{l}")
        conv_s.append(c_new)

    yp = _norm_drop_lead(xp, final_norm_w, batch, CHUNK, seq, x_prompt.dtype, "norm_out_prompt")
    tm = _pick_tile(n_seq * toks, 512)
    ys = _norm(xs, final_norm_w, x_sample.dtype, tm, (n_seq * toks // tm,), lambda i: (i, 0), n_seq * toks,
               lambda i: (i, 0), "norm_out_sample").reshape(n_seq, toks, d)
    return (yp, ys, jnp.stack(conv_p).astype(x_prompt.dtype), jnp.stack(gdn_p).astype(state_gdn.dtype),
            jnp.stack(hg_p).astype(state_hgrn.dtype), jnp.stack(conv_s).astype(state_gdn_conv.dtype),
            carried[0].astype(state_gdn.dtype), carried[1].astype(state_hgrn.dtype))
```
